```python
import math
import jax, jax.numpy as jnp
from jax import lax
import numpy as np

D_MODEL = 2048
BATCH = 4
SEQ = 2048
DEPTH = 2
DEC_BATCH = 128
DEC_SEQ = 1
PAST_LEN = 16384
PAGE_SIZE = 128

N_RG_LAYERS = (DEPTH + 1) // 2
N_SSD_LAYERS = DEPTH // 2
CONV_W = 4
EPS = 1e-6
D_RNN = D_MODEL
RG_BLOCKS = 8
RG_BLOCK = D_RNN // RG_BLOCKS
RG_C = 8.0
SSD_EXPAND = 2
D_INNER = SSD_EXPAND * D_MODEL
SSD_HEAD_DIM = 64
SSD_HEADS = D_INNER // SSD_HEAD_DIM
SSD_GROUPS = 8
SSD_HPG = SSD_HEADS // SSD_GROUPS
SSD_STATE = 128
SSD_CONV_DIM = D_INNER + 2 * SSD_GROUPS * SSD_STATE
SSD_IN_DIM = D_INNER + SSD_CONV_DIM + SSD_HEADS
SSD_CHUNK = 128
PEER_HEADS = 8
PEER_NKEYS = 128
PEER_EXPERTS = PEER_NKEYS * PEER_NKEYS
PEER_DKEY = 256
PEER_HALF = PEER_DKEY // 2
PEER_TOPK = 16
PEER_TOKEN_BLOCK = 128

kernel_name = 'hybrid_rglru_ssd_peer_adaln_step'


def rmsnorm(x, g):
    xf = x.astype(jnp.float32)
    y = xf * lax.rsqrt(jnp.mean(xf * xf, axis=-1, keepdims=True) + EPS)
    return (y * g.astype(jnp.float32)).astype(x.dtype)


def causal_dwconv(x, buf, w, b):
    L = x.shape[1]
    xp = jnp.concatenate([buf.astype(x.dtype), x], axis=1)
    y = b + w[0] * xp[:, 0:L]
    for k in range(1, CONV_W):
        y = y + w[k] * xp[:, k:k + L]
    return y, xp[:, -(CONV_W - 1):]


def rglru_mixer(h, conv_buf, h0, w_in, b_in, conv_w, conv_b, w_a, b_a, w_i, b_i, lam, w_out, b_out):
    f32 = jnp.float32
    Bn, L, _ = h.shape
    proj = h @ w_in + b_in
    gate_branch, xr = jnp.split(proj, 2, axis=-1)
    xc, new_buf = causal_dwconv(xr, conv_buf, conv_w, conv_b)
    xb = xc.reshape(Bn, L, RG_BLOCKS, RG_BLOCK)
    r = jax.nn.sigmoid(jnp.einsum('blki,kij->blkj', xb, w_a).reshape(Bn, L, D_RNN) + b_a)
    i = jax.nn.sigmoid(jnp.einsum('blki,kij->blkj', xb, w_i).reshape(Bn, L, D_RNN) + b_i)
    log_a = (-RG_C * jax.nn.softplus(-lam.astype(f32))) * r.astype(f32)
    a = jnp.exp(log_a)
    mult = jnp.sqrt(jnp.maximum(-jnp.expm1(2.0 * log_a), 0.0))
    bterm = mult * (i * xc).astype(f32)
    bterm = bterm.at[:, 0].add(a[:, 0] * h0.astype(f32))

    def combine(lhs, rhs):
        a1, b1 = lhs
        a2, b2 = rhs
        return a1 * a2, a2 * b1 + b2

    _, hs = lax.associative_scan(combine, (a, bterm), axis=1)
    y = hs.astype(h.dtype) * jax.nn.gelu(gate_branch)
    return y @ w_out + b_out, new_buf, hs[:, -1].astype(h.dtype)


def ssd_scan(x, dt, A, bmat, cmat, s0):
    f32 = jnp.float32
    Bn, L = x.shape[:2]
    Q = min(SSD_CHUNK, L)
    nc = -(-L // Q)
    pad = nc * Q - L
    x, dt, bmat, cmat = [t.astype(f32) for t in (x, dt, bmat, cmat)]
    if pad:
        padf = lambda t: jnp.pad(t, [(0, 0), (0, pad)] + [(0, 0)] * (t.ndim - 2))
        x, dt, bmat, cmat = padf(x), padf(dt), padf(bmat), padf(cmat)

    def chunks(t):
        return jnp.moveaxis(t.reshape((Bn, nc, Q) + t.shape[2:]), 1, 0)

    tri = jnp.tril(jnp.ones((Q, Q), dtype=bool))

    def step(s, inp):
        xq, dq, bq, cq = inp
        cum = jnp.cumsum(dq * A, axis=1)
        seg = cum[:, :, None] - cum[:, None, :]
        decay = jnp.exp(jnp.where(tri[None, :, :, None, None], seg, -jnp.inf))
        cb = jnp.einsum('btgn,bsgn->btsg', cq, bq)
        w = decay * cb[..., None] * dq[:, None]
        y = jnp.einsum('btsge,bsgep->btgep', w, xq)
        y = y + jnp.einsum('btgn,bgepn->btgep', cq, s) * jnp.exp(cum)[..., None]
        last = cum[:, -1]
        wend = jnp.exp(last[:, None] - cum) * dq
        s = jnp.exp(last)[..., None, None] * s + jnp.einsum('bsge,bsgn,bsgep->bgepn', wend, bq, xq)
        return s, y

    s, ys = lax.scan(step, s0.astype(f32), (chunks(x), chunks(dt), chunks(bmat), chunks(cmat)))
    y = jnp.moveaxis(ys, 0, 1).reshape((Bn, nc * Q) + ys.shape[3:])[:, :L]
    return y, s


def ssd_mixer(h, conv_buf, s0, w_in, conv_w, conv_b, dt_bias, a_log, d_skip, norm_g, w_out):
    f32 = jnp.float32
    Bn, L, _ = h.shape
    proj = h @ w_in
    z, xbc, dt_raw = jnp.split(proj, [D_INNER, D_INNER + SSD_CONV_DIM], axis=-1)
    xbc, new_buf = causal_dwconv(xbc, conv_buf, conv_w, conv_b)
    xbc = jax.nn.silu(xbc)
    xs, bmat, cmat = jnp.split(xbc, [D_INNER, D_INNER + SSD_GROUPS * SSD_STATE], axis=-1)
    dt = jax.nn.softplus(dt_raw.astype(f32) + dt_bias.astype(f32))
    A = -jnp.exp(a_log.astype(f32))
    xs5 = xs.reshape(Bn, L, SSD_GROUPS, SSD_HPG, SSD_HEAD_DIM)
    y, s_new = ssd_scan(xs5, dt.reshape(Bn, L, SSD_GROUPS, SSD_HPG), A.reshape(SSD_GROUPS, SSD_HPG),
                        bmat.reshape(Bn, L, SSD_GROUPS, SSD_STATE), cmat.reshape(Bn, L, SSD_GROUPS, SSD_STATE), s0)
    y = y + d_skip.astype(f32).reshape(SSD_GROUPS, SSD_HPG, 1) * xs5.astype(f32)
    y = y.reshape(Bn, L, D_INNER).astype(h.dtype)
    y = rmsnorm(y * jax.nn.silu(z), norm_g)
    return y @ w_out, new_buf, s_new.astype(h.dtype)


def peer_mixer(h, w_q, k1, k2, u, v):
    f32 = jnp.float32
    Bn, L, D = h.shape
    T = Bn * L
    xt = h.reshape(T, D)
    q = (xt @ w_q).reshape(T, PEER_HEADS, 2, PEER_HALF).astype(f32)
    s1 = jnp.einsum('thd,kd->thk', q[:, :, 0], k1.astype(f32))
    s2 = jnp.einsum('thd,kd->thk', q[:, :, 1], k2.astype(f32))
    v1, i1 = lax.top_k(s1, PEER_TOPK)
    v2, i2 = lax.top_k(s2, PEER_TOPK)
    cand = (v1[..., :, None] + v2[..., None, :]).reshape(T, PEER_HEADS, PEER_TOPK * PEER_TOPK)
    cidx = (i1[..., :, None] * PEER_NKEYS + i2[..., None, :]).reshape(T, PEER_HEADS, PEER_TOPK * PEER_TOPK)
    sc, pos = lax.top_k(cand, PEER_TOPK)
    idx = jnp.take_along_axis(cidx, pos, axis=-1)
    g = jax.nn.softmax(sc, axis=-1).astype(h.dtype)
    tb = min(PEER_TOKEN_BLOCK, T)
    nb = -(-T // tb)
    pad = nb * tb - T
    xpad = jnp.pad(xt, [(0, pad), (0, 0)]).reshape(nb, tb, D)
    ipad = jnp.pad(idx, [(0, pad), (0, 0), (0, 0)]).reshape(nb, tb, PEER_HEADS, PEER_TOPK)
    gpad = jnp.pad(g, [(0, pad), (0, 0), (0, 0)]).reshape(nb, tb, PEER_HEADS, PEER_TOPK)

    def block(args):
        xb, ib, gb = args
        act = jax.nn.gelu(jnp.einsum('td,thkd->thk', xb, u[ib]))
        return jnp.einsum('thk,thkd->td', act * gb, v[ib])

    out = lax.map(block, (xpad, ipad, gpad)).reshape(nb * tb, D)[:T]
    return out.reshape(Bn, L, D)


def run_trunk(x, c, rg_conv, rg_h, ssd_conv, ssd_s, p):
    rg_conv_new, rg_h_new, ssd_conv_new, ssd_new = [], [], [], []
    cs = jax.nn.silu(c)
    for l in range(DEPTH):
        mod = cs @ p['w_mod'][l] + p['b_mod'][l]
        sh1, sc1, g1, sh2, sc2, g2 = [m[:, None, :] for m in jnp.split(mod, 6, axis=-1)]
        hm = rmsnorm(x, p['norm1_g'][l]) * (1 + sc1) + sh1
        j = l // 2
        if l % 2 == 0:
            out, cb, hs = rglru_mixer(hm, rg_conv[j], rg_h[j], p['rg_w_in'][j], p['rg_b_in'][j],
                                      p['rg_conv_w'][j], p['rg_conv_b'][j], p['rg_w_a'][j], p['rg_b_a'][j],
                                      p['rg_w_i'][j], p['rg_b_i'][j], p['rg_lambda'][j],
                                      p['rg_w_out'][j], p['rg_b_out'][j])
            rg_conv_new.append(cb)
            rg_h_new.append(hs)
        else:
            out, cb, ss = ssd_mixer(hm, ssd_conv[j], ssd_s[j], p['ssd_w_in'][j], p['ssd_conv_w'][j],
                                    p['ssd_conv_b'][j], p['ssd_dt_bias'][j], p['ssd_a_log'][j],
                                    p['ssd_d'][j], p['ssd_norm_g'][j], p['ssd_w_out'][j])
            ssd_conv_new.append(cb)
            ssd_new.append(ss)
        x = x + g1 * out
        hc = rmsnorm(x, p['norm2_g'][l]) * (1 + sc2) + sh2
        x = x + g2 * peer_mixer(hc, p['peer_w_q'][l], p['peer_k1'][l], p['peer_k2'][l],
                                p['peer_u'][l], p['peer_v'][l])
    y = rmsnorm(x, p['final_g'])
    return y, jnp.stack(rg_conv_new), jnp.stack(rg_h_new), jnp.stack(ssd_conv_new), jnp.stack(ssd_new)


def setup_inputs(seed: int = 0) -> dict:
    key = jax.random.key(seed)
    ks = jax.random.split(key, 64)
    cnt = iter(range(64))
    f32 = jnp.float32

    def nrm(shape, scale):
        return jax.random.normal(ks[next(cnt)], shape, f32) * scale

    def unif(shape, lo, hi):
        return jax.random.uniform(ks[next(cnt)], shape, f32, lo, hi)

    d = D_MODEL
    out = {}
    out['x_prompt'] = nrm((BATCH, SEQ, d), 1.0)
    out['x_sample'] = nrm((DEC_BATCH, DEC_SEQ, d), 1.0)
    out['state_rg_conv'] = nrm((N_RG_LAYERS, DEC_BATCH, CONV_W - 1, D_RNN), 1.0)
    out['state_rg_h'] = nrm((N_RG_LAYERS, DEC_BATCH, D_RNN), 0.5)
    out['state_ssd_conv'] = nrm((N_SSD_LAYERS, DEC_BATCH, CONV_W - 1, SSD_CONV_DIM), 1.0)
    out['state_ssd'] = nrm((N_SSD_LAYERS, DEC_BATCH, SSD_GROUPS, SSD_HPG, SSD_HEAD_DIM, SSD_STATE), 0.1)
    out['c_prompt'] = nrm((BATCH, d), 1.0)
    out['c_sample'] = nrm((DEC_BATCH, d), 1.0)
    out['norm1_g'] = 1.0 + nrm((DEPTH, d), 0.05)
    out['norm2_g'] = 1.0 + nrm((DEPTH, d), 0.05)
    out['w_mod'] = nrm((DEPTH, d, 6 * d), 0.5 * d ** -0.5)
    out['b_mod'] = nrm((DEPTH, 6 * d), 0.02)
    out['rg_w_in'] = nrm((N_RG_LAYERS, d, 2 * D_RNN), d ** -0.5)
    out['rg_b_in'] = nrm((N_RG_LAYERS, 2 * D_RNN), 0.02)
    out['rg_conv_w'] = nrm((N_RG_LAYERS, CONV_W, D_RNN), CONV_W ** -0.5)
    out['rg_conv_b'] = nrm((N_RG_LAYERS, D_RNN), 0.02)
    out['rg_w_a'] = nrm((N_RG_LAYERS, RG_BLOCKS, RG_BLOCK, RG_BLOCK), RG_BLOCK ** -0.5)
    out['rg_b_a'] = nrm((N_RG_LAYERS, D_RNN), 0.02)
    out['rg_w_i'] = nrm((N_RG_LAYERS, RG_BLOCKS, RG_BLOCK, RG_BLOCK), RG_BLOCK ** -0.5)
    out['rg_b_i'] = nrm((N_RG_LAYERS, D_RNN), 0.02)
    a_base = unif((N_RG_LAYERS, D_RNN), 0.9, 0.999) ** (1.0 / RG_C)
    out['rg_lambda'] = jnp.log(a_base) - jnp.log1p(-a_base)
    out['rg_w_out'] = nrm((N_RG_LAYERS, D_RNN, d), D_RNN ** -0.5)
    out['rg_b_out'] = nrm((N_RG_LAYERS, d), 0.02)
    out['ssd_w_in'] = nrm((N_SSD_LAYERS, d, SSD_IN_DIM), d ** -0.5)
    out['ssd_conv_w'] = nrm((N_SSD_LAYERS, CONV_W, SSD_CONV_DIM), CONV_W ** -0.5)
    out['ssd_conv_b'] = nrm((N_SSD_LAYERS, SSD_CONV_DIM), 0.02)
    dt0 = jnp.exp(unif((N_SSD_LAYERS, SSD_HEADS), math.log(1e-3), math.log(1e-1)))
    out['ssd_dt_bias'] = dt0 + jnp.log(-jnp.expm1(-dt0))
    out['ssd_a_log'] = jnp.log(unif((N_SSD_LAYERS, SSD_HEADS), 1.0, 16.0))
    out['ssd_d'] = 1.0 + nrm((N_SSD_LAYERS, SSD_HEADS), 0.1)
    out['ssd_norm_g'] = 1.0 + nrm((N_SSD_LAYERS, D_INNER), 0.05)
    out['ssd_w_out'] = nrm((N_SSD_LAYERS, D_INNER, d), D_INNER ** -0.5)
    out['peer_w_q'] = nrm((DEPTH, d, PEER_HEADS * PEER_DKEY), d ** -0.5)
    out['peer_k1'] = nrm((DEPTH, PEER_NKEYS, PEER_HALF), PEER_HALF ** -0.5)
    out['peer_k2'] = nrm((DEPTH, PEER_NKEYS, PEER_HALF), PEER_HALF ** -0.5)
    out['peer_u'] = nrm((DEPTH, PEER_EXPERTS, d), d ** -0.5)
    out['peer_v'] = nrm((DEPTH, PEER_EXPERTS, d), PEER_HEADS ** -0.5)
    out['final_g'] = 1.0 + nrm((d,), 0.05)
    return out


def reference(x_prompt, x_sample, state_rg_conv, state_rg_h, state_ssd_conv, state_ssd, c_prompt, c_sample,
              norm1_g, norm2_g, w_mod, b_mod,
              rg_w_in, rg_b_in, rg_conv_w, rg_conv_b, rg_w_a, rg_b_a, rg_w_i, rg_b_i, rg_lambda, rg_w_out, rg_b_out,
              ssd_w_in, ssd_conv_w, ssd_conv_b, ssd_dt_bias, ssd_a_log, ssd_d, ssd_norm_g, ssd_w_out,
              peer_w_q, peer_k1, peer_k2, peer_u, peer_v, final_g):
    p = dict(norm1_g=norm1_g, norm2_g=norm2_g, w_mod=w_mod, b_mod=b_mod,
             rg_w_in=rg_w_in, rg_b_in=rg_b_in, rg_conv_w=rg_conv_w, rg_conv_b=rg_conv_b,
             rg_w_a=rg_w_a, rg_b_a=rg_b_a, rg_w_i=rg_w_i, rg_b_i=rg_b_i, rg_lambda=rg_lambda,
             rg_w_out=rg_w_out, rg_b_out=rg_b_out,
             ssd_w_in=ssd_w_in, ssd_conv_w=ssd_conv_w, ssd_conv_b=ssd_conv_b, ssd_dt_bias=ssd_dt_bias,
             ssd_a_log=ssd_a_log, ssd_d=ssd_d, ssd_norm_g=ssd_norm_g, ssd_w_out=ssd_w_out,
             peer_w_q=peer_w_q, peer_k1=peer_k1, peer_k2=peer_k2, peer_u=peer_u, peer_v=peer_v,
             final_g=final_g)
    bp = x_prompt.shape[0]
    dtp = x_prompt.dtype
    z_rg_conv = jnp.zeros((N_RG_LAYERS, bp, CONV_W - 1, D_RNN), dtp)
    z_rg_h = jnp.zeros((N_RG_LAYERS, bp, D_RNN), dtp)
    z_ssd_conv = jnp.zeros((N_SSD_LAYERS, bp, CONV_W - 1, SSD_CONV_DIM), dtp)
    z_ssd = jnp.zeros((N_SSD_LAYERS, bp, SSD_GROUPS, SSD_HPG, SSD_HEAD_DIM, SSD_STATE), dtp)
    y_prompt, p_rg_conv, p_rg_h, p_ssd_conv, p_ssd = run_trunk(
        x_prompt, c_prompt, z_rg_conv, z_rg_h, z_ssd_conv, z_ssd, p)
    y_sample, s_rg_conv, s_rg_h, s_ssd_conv, s_ssd = run_trunk(
        x_sample, c_sample, state_rg_conv, state_rg_h, state_ssd_conv, state_ssd, p)
    return (y_prompt, y_sample, p_rg_conv, p_rg_h, p_ssd_conv, p_ssd, s_rg_conv, s_rg_h, s_ssd_conv, s_ssd)
```

```python
import functools
import math

import jax
import jax.numpy as jnp
from jax import lax
from jax.experimental import pallas as pl
from jax.experimental.pallas import tpu as pltpu

F32 = jnp.float32
BF16 = jnp.bfloat16

EPS = 1e-6
CONV_W = 4
RG_BLOCKS = 8
RG_C = 8.0
SSD_HEAD_DIM = 64
SSD_GROUPS = 8
SSD_STATE = 128
SSD_CHUNK = 128
PEER_HEADS = 8
PEER_NKEYS = 128
PEER_TOPK = 16
LANES = 128
SUBLANES = 8
VMEM_LIMIT = 56 * 1024 * 1024
NEG_INF = float("-inf")


def _params(*sem):
    return pltpu.CompilerParams(dimension_semantics=sem, vmem_limit_bytes=VMEM_LIMIT)


def _silu(x):
    return x * jax.nn.sigmoid(x)


def _softplus(x):
    return jnp.maximum(x, 0.0) + jnp.log1p(jnp.exp(-jnp.abs(x)))


def _rms(x):
    return x * lax.rsqrt(jnp.mean(x * x, axis=-1, keepdims=True) + EPS)


def _dot(a, b):
    return jnp.dot(a, b, preferred_element_type=F32)


def _dot_nt(a, b):
    return lax.dot_general(a, b, (((1,), (1,)), ((), ())), preferred_element_type=F32)


def _dot_tn(a, b):
    return lax.dot_general(a, b, (((0,), (0,)), ((), ())), preferred_element_type=F32)


def _mod_kernel(c_ref, w_ref, b_ref, o_ref):
    cs = _silu(c_ref[...]).astype(BF16)
    o_ref[...] = _dot(cs, w_ref[...].astype(BF16)) + b_ref[...]


def _modulation(c_all, w_mod, b_mod):
    depth, d, n = w_mod.shape
    rows = c_all.shape[0]
    tn = 512
    return pl.pallas_call(
        _mod_kernel,
        grid=(depth, n // tn),
        in_specs=[
            pl.BlockSpec((rows, d), lambda l, j: (0, 0)),
            pl.BlockSpec((None, d, tn), lambda l, j: (l, 0, j)),
            pl.BlockSpec((None, 1, tn), lambda l, j: (l, 0, j)),
        ],
        out_specs=pl.BlockSpec((None, rows, tn), lambda l, j: (l, 0, j)),
        out_shape=jax.ShapeDtypeStruct((depth, rows, n), F32),
        compiler_params=_params("arbitrary", "arbitrary"),
        name="modulation",
    )(c_all, w_mod, b_mod.reshape(depth, 1, n))


def _proj_kernel(*refs, pro, epi, emit_lhs, n_extra, out_dtype):
    it = iter(refs)
    if pro == "normmod":
        x_ref, g_ref, sc_ref, sh_ref = next(it), next(it), next(it), next(it)
    elif pro == "gated":
        y_ref, z_ref, g_ref = next(it), next(it), next(it)
    else:
        x_ref = next(it)
    w_ref = next(it)
    b_ref = next(it) if epi in ("bias", "resid") else None
    if epi == "resid":
        xres_ref, gate_ref = next(it), next(it)
    wx_ref = next(it) if n_extra else None
    o_ref = next(it)
    lhs_out_ref = next(it) if emit_lhs else None
    ox_ref = next(it) if n_extra else None
    lhs_ref = next(it) if pro != "plain" else None

    j = pl.program_id(1)

    if pro != "plain":
        @pl.when(j == 0)
        def _():
            if pro == "normmod":
                x = x_ref[...]
                h = _rms(x) * g_ref[...] * (1.0 + sc_ref[...]) + sh_ref[...]
            else:
                y = y_ref[...]
                h = _rms(y * _silu(z_ref[...])) * g_ref[...]
            hb = h.astype(BF16)
            lhs_ref[...] = hb
            if emit_lhs:
                lhs_out_ref[...] = hb
            if n_extra:
                ox_ref[...] = _dot(hb, wx_ref[...])
        lhs = lhs_ref[...]
    else:
        lhs = x_ref[...]

    acc = _dot(lhs, w_ref[...])
    if b_ref is not None:
        acc = acc + b_ref[...]
    if epi == "resid":
        acc = xres_ref[...] + gate_ref[...] * acc
    o_ref[...] = acc.astype(out_dtype)


def _projection(pro, epi, ins, w, *, tm, tn, rows_per_mod=None, bias=None, resid=None,
                w_extra=None, emit_lhs=False, out_dtype=F32, name="proj"):
    k, n = w.shape
    m = ins[0].shape[0]
    assert m % tm == 0 and n % tn == 0
    grid = (m // tm, n // tn)

    def mod_spec(width_block, col_of):
        if rows_per_mod == 1:
            return pl.BlockSpec((tm, width_block), lambda i, j: (i, col_of(j)))
        assert rows_per_mod % tm == 0
        t = rows_per_mod // tm
        return pl.BlockSpec((None, 1, width_block), lambda i, j: (i // t, 0, col_of(j)))

    def mod_arr(a):
        return a if rows_per_mod == 1 else a.reshape(a.shape[0], 1, a.shape[1])

    args, specs = [], []
    if pro == "normmod":
        x, g, sc, sh = ins
        args += [x, g.reshape(1, k), mod_arr(sc), mod_arr(sh)]
        specs += [pl.BlockSpec((tm, k), lambda i, j: (i, 0)),
                  pl.BlockSpec((1, k), lambda i, j: (0, 0)),
                  mod_spec(k, lambda j: 0), mod_spec(k, lambda j: 0)]
    elif pro == "gated":
        y, z, g = ins
        args += [y, z, g.reshape(1, k)]
        specs += [pl.BlockSpec((tm, k), lambda i, j: (i, 0)),
                  pl.BlockSpec((tm, k), lambda i, j: (i, 0)),
                  pl.BlockSpec((1, k), lambda i, j: (0, 0))]
    else:
        (x,) = ins
        args += [x]
        specs += [pl.BlockSpec((tm, k), lambda i, j: (i, 0))]
    args.append(w)
    specs.append(pl.BlockSpec((k, tn), lambda i, j: (0, j)))
    if epi in ("bias", "resid"):
        b = bias if bias is not None else jnp.zeros((n,), F32)
        args.append(b.reshape(1, n))
        specs.append(pl.BlockSpec((1, tn), lambda i, j: (0, j)))
    if epi == "resid":
        xres, gate = resid
        args += [xres, mod_arr(gate)]
        specs += [pl.BlockSpec((tm, tn), lambda i, j: (i, j)), mod_spec(tn, lambda j: j)]
    n_extra = 0
    if w_extra is not None:
        n_extra = w_extra.shape[1]
        args.append(w_extra)
        specs.append(pl.BlockSpec((k, n_extra), lambda i, j: (0, 0)))

    out_shapes = [jax.ShapeDtypeStruct((m, n), out_dtype)]
    out_specs = [pl.BlockSpec((tm, tn), lambda i, j: (i, j))]
    if emit_lhs:
        out_shapes.append(jax.ShapeDtypeStruct((m, k), BF16))
        out_specs.append(pl.BlockSpec((tm, k), lambda i, j: (i, 0)))
    if n_extra:
        out_shapes.append(jax.ShapeDtypeStruct((m, n_extra), F32))
        out_specs.append(pl.BlockSpec((tm, n_extra), lambda i, j: (i, 0)))
    scratch = [pltpu.VMEM((tm, k), BF16)] if pro != "plain" else []

    outs = pl.pallas_call(
        functools.partial(_proj_kernel, pro=pro, epi=epi, emit_lhs=emit_lhs, n_extra=n_extra,
                          out_dtype=out_dtype),
        grid=grid, in_specs=specs, out_specs=out_specs, out_shape=out_shapes,
        scratch_shapes=scratch,
        compiler_params=_params("arbitrary", "arbitrary"),
        name=name,
    )(*args)
    return outs if len(outs) > 1 else outs[0]


def _rglru_gates(xc, wa_ref, ba_ref, wi_ref, bi_ref, lam_ref):
    xcb = xc.astype(BF16)
    r = jax.nn.sigmoid(_dot(xcb, wa_ref[...]) + ba_ref[...])
    i = jax.nn.sigmoid(_dot(xcb, wi_ref[...]) + bi_ref[...])
    log_a = (-RG_C * _softplus(-lam_ref[...])) * r
    a = jnp.exp(log_a)
    mult = jnp.sqrt(jnp.maximum(-jnp.tanh(log_a) * (a * a + 1.0), 0.0))
    return a, mult * (i * xc)


RG_ROWS = 256


def _rglru_seq_kernel(gate_ref, xr_ref, cw_ref, cb_ref, wa_ref, ba_ref, wi_ref, bi_ref, lam_ref,
                      y_ref, hlast_ref, xpad_ref):
    seq, blk = xr_ref.shape
    pad = SUBLANES
    xpad_ref[0:pad, :] = jnp.zeros((pad, blk), F32)
    xpad_ref[pad:pad + seq, :] = xr_ref[...]
    row_in_tile = lax.broadcasted_iota(jnp.int32, (RG_ROWS, blk), 0) % SUBLANES

    def step(c, carry):
        start = pl.multiple_of(c * RG_ROWS, RG_ROWS)
        xe = xpad_ref[pl.ds(start, RG_ROWS + pad), :]
        xc = cb_ref[...] + cw_ref[3:4, :] * xe[pad:]
        for k in range(1, CONV_W):
            xc = xc + cw_ref[3 - k:4 - k, :] * pltpu.roll(xe, k, axis=0)[pad:]
        a, b = _rglru_gates(xc, wa_ref, ba_ref, wi_ref, bi_ref, lam_ref)
        for s in (1, 2, 4):
            m = row_in_tile >= s
            b = jnp.where(m, a * pltpu.roll(b, s, axis=0) + b, b)
            a = jnp.where(m, a * pltpu.roll(a, s, axis=0), a)
        hs = []
        for t in range(RG_ROWS // SUBLANES):
            sl = slice(t * SUBLANES, (t + 1) * SUBLANES)
            h = b[sl] + a[sl] * carry
            hs.append(h)
            carry = h[SUBLANES - 1:SUBLANES, :]
        h_all = jnp.concatenate(hs, axis=0)
        y_ref[pl.ds(start, RG_ROWS), :] = (h_all * jax.nn.gelu(gate_ref[pl.ds(start, RG_ROWS), :])).astype(BF16)
        return carry

    carry = lax.fori_loop(0, seq // RG_ROWS, step, jnp.zeros((1, blk), F32))
    hlast_ref[...] = carry


def _rglru_seq(proj, conv_w, conv_b, w_a, b_a, w_i, b_i, lam):
    nb, seq, two_d = proj.shape
    d = two_d // 2
    blk = d // RG_BLOCKS
    vec = lambda a: a.reshape(1, d)
    vspec = pl.BlockSpec((1, blk), lambda b, k: (0, k))
    wspec = pl.BlockSpec((None, blk, blk), lambda b, k: (k, 0, 0))
    return pl.pallas_call(
        _rglru_seq_kernel,
        grid=(nb, RG_BLOCKS),
        in_specs=[
            pl.BlockSpec((None, seq, blk), lambda b, k: (b, 0, k)),
            pl.BlockSpec((None, seq, blk), lambda b, k: (b, 0, RG_BLOCKS + k)),
            pl.BlockSpec((CONV_W, blk), lambda b, k: (0, k)),
            vspec, wspec, vspec, wspec, vspec, vspec,
        ],
        out_specs=[
            pl.BlockSpec((None, seq, blk), lambda b, k: (b, 0, k)),
            pl.BlockSpec((None, 1, blk), lambda b, k: (b, 0, k)),
        ],
        out_shape=[jax.ShapeDtypeStruct((nb, seq, d), BF16), jax.ShapeDtypeStruct((nb, 1, d), F32)],
        scratch_shapes=[pltpu.VMEM((seq + SUBLANES, blk), F32)],
        compiler_params=_params("arbitrary", "arbitrary"),
        name="rglru_seq",
    )(proj, proj, conv_w, vec(conv_b), w_a.astype(BF16), vec(b_a), w_i.astype(BF16), vec(b_i), vec(lam))


def _rglru_step_kernel(gate_ref, xr_ref, c0_ref, c1_ref, c2_ref, h0_ref, cw_ref, cb_ref,
                       wa_ref, ba_ref, wi_ref, bi_ref, lam_ref, y_ref, h_ref):
    xc = (cb_ref[...] + cw_ref[0:1, :] * c0_ref[...] + cw_ref[1:2, :] * c1_ref[...]
          + cw_ref[2:3, :] * c2_ref[...] + cw_ref[3:4, :] * xr_ref[...])
    a, b = _rglru_gates(xc, wa_ref, ba_ref, wi_ref, bi_ref, lam_ref)
    h = a * h0_ref[...] + b
    h_ref[...] = h
    y_ref[...] = (h * jax.nn.gelu(gate_ref[...])).astype(BF16)


def _rglru_step(proj, conv_state, h0, conv_w, conv_b, w_a, b_a, w_i, b_i, lam):
    nb, two_d = proj.shape
    d = two_d // 2
    blk = d // RG_BLOCKS
    vec = lambda a: a.reshape(1, d)
    rows = pl.BlockSpec((nb, blk), lambda k: (0, k))
    vspec = pl.BlockSpec((1, blk), lambda k: (0, k))
    wspec = pl.BlockSpec((None, blk, blk), lambda k: (k, 0, 0))
    return pl.pallas_call(
        _rglru_step_kernel,
        grid=(RG_BLOCKS,),
        in_specs=[rows, pl.BlockSpec((nb, blk), lambda k: (0, RG_BLOCKS + k)), rows, rows, rows, rows,
                  pl.BlockSpec((CONV_W, blk), lambda k: (0, k)),
                  vspec, wspec, vspec, wspec, vspec, vspec],
        out_specs=[rows, rows],
        out_shape=[jax.ShapeDtypeStruct((nb, d), BF16), jax.ShapeDtypeStruct((nb, d), F32)],
        compiler_params=_params("arbitrary"),
        name="rglru_step",
    )(proj, proj, conv_state[:, 0], conv_state[:, 1], conv_state[:, 2], h0, conv_w, vec(conv_b),
      w_a.astype(BF16), vec(b_a), w_i.astype(BF16), vec(b_i), vec(lam))


def _split3(x):
    hi = x.astype(BF16)
    r1 = x - hi.astype(F32)
    mid = r1.astype(BF16)
    lo = (r1 - mid.astype(F32)).astype(BF16)
    return hi, mid, lo


def _ssd_seq_kernel(xbc_ref, halo_ref, dt_ref, cw_ref, cb_ref, dtb_ref, alog_ref, dskip_ref,
                    y_ref, st_ref, *, d_inner):
    c = pl.program_id(1)
    q = SSD_CHUNK
    p = SSD_HEAD_DIM
    n = SSD_STATE
    hpg = d_inner // p // SSD_GROUPS
    pad = SUBLANES

    @pl.when(c == 0)
    def _():
        st_ref[...] = jnp.zeros(st_ref.shape, F32)

    halo = jnp.where(c > 0, halo_ref[...], 0.0)
    xe = jnp.concatenate([halo, xbc_ref[...]], axis=0)
    xc = cb_ref[...] + cw_ref[3:4, :] * xe[pad:]
    for k in range(1, CONV_W):
        xc = xc + cw_ref[3 - k:4 - k, :] * pltpu.roll(xe, k, axis=0)[pad:]
    xc = _silu(xc)

    dt = _softplus(dt_ref[...] + dtb_ref[...])
    da = dt * (-jnp.exp(alog_ref[...]))
    ri = lax.broadcasted_iota(jnp.int32, (q, q), 0)
    ci = lax.broadcasted_iota(jnp.int32, (q, q), 1)
    tri = ri >= ci
    ltri = jnp.where(tri, 1.0, 0.0).astype(BF16)
    hi, mid, lo = _split3(da)
    cum = _dot(ltri, hi) + _dot(ltri, mid) + _dot(ltri, lo)
    cum_t = cum.T
    dt_t = dt.T
    last = cum[q - 1:q, :]
    ecum = jnp.exp(cum)
    wend = jnp.exp(last - cum) * dt
    elast = jnp.exp(last)
    lane = lax.broadcasted_iota(jnp.int32, (1, hpg * p), 1) // p

    for g in range(SSD_GROUPS):
        bb = xc[:, d_inner + g * n:d_inner + (g + 1) * n].astype(BF16)
        cc = xc[:, d_inner + SSD_GROUPS * n + g * n:d_inner + SSD_GROUPS * n + (g + 1) * n].astype(BF16)
        cb = _dot_nt(cc, bb)
        st = st_ref[g]
        ys = _dot(cc, st.astype(BF16))
        xw_parts = []
        elrow = jnp.zeros((1, hpg * p), F32)
        y_parts = []
        for e in range(hpg):
            h = g * hpg + e
            col = cum[:, h:h + 1]
            decay = jnp.exp(jnp.where(tri, col - cum_t[h:h + 1, :], NEG_INF))
            w = decay * cb * dt_t[h:h + 1, :]
            xh = xc[:, h * p:(h + 1) * p]
            yh = (_dot(w.astype(BF16), xh.astype(BF16)) + ys[:, e * p:(e + 1) * p] * ecum[:, h:h + 1]
                  + dskip_ref[:, h:h + 1] * xh)
            y_parts.append(yh)
            xw_parts.append(xh * wend[:, h:h + 1])
            elrow = jnp.where(lane == e, elast[:, h:h + 1], elrow)
        y_ref[:, g * hpg * p:(g + 1) * hpg * p] = jnp.concatenate(y_parts, axis=1)
        xw = jnp.concatenate(xw_parts, axis=1).astype(BF16)
        st_ref[g] = st * elrow + _dot_tn(bb, xw)


def _ssd_seq(xbc, dt_raw, conv_w, conv_b, dt_bias, a_log, d_skip, d_inner):
    nb, seq, cdim = xbc.shape
    q = SSD_CHUNK
    hp = dt_raw.shape[-1]
    heads = d_inner // SSD_HEAD_DIM
    epp = d_inner // SSD_GROUPS
    padv = lambda a: jnp.pad(a.reshape(1, heads), ((0, 0), (0, hp - heads)))
    halo_blocks = q // SUBLANES
    return pl.pallas_call(
        functools.partial(_ssd_seq_kernel, d_inner=d_inner),
        grid=(nb, seq // q),
        in_specs=[
            pl.BlockSpec((None, q, cdim), lambda b, c: (b, c, 0)),
            pl.BlockSpec((None, SUBLANES, cdim), lambda b, c: (b, jnp.maximum(c * halo_blocks - 1, 0), 0)),
            pl.BlockSpec((None, q, hp), lambda b, c: (b, c, 0)),
            pl.BlockSpec((CONV_W, cdim), lambda b, c: (0, 0)),
            pl.BlockSpec((1, cdim), lambda b, c: (0, 0)),
            pl.BlockSpec((1, hp), lambda b, c: (0, 0)),
            pl.BlockSpec((1, hp), lambda b, c: (0, 0)),
            pl.BlockSpec((1, hp), lambda b, c: (0, 0)),
        ],
        out_specs=[
            pl.BlockSpec((None, q, d_inner), lambda b, c: (b, c, 0)),
            pl.BlockSpec((None, SSD_GROUPS, SSD_STATE, epp), lambda b, c: (b, 0, 0, 0)),
        ],
        out_shape=[jax.ShapeDtypeStruct((nb, seq, d_inner), F32),
                   jax.ShapeDtypeStruct((nb, SSD_GROUPS, SSD_STATE, epp), F32)],
        compiler_params=_params("arbitrary", "arbitrary"),
        name="ssd_seq",
    )(xbc, xbc, dt_raw, conv_w, conv_b.reshape(1, cdim), padv(dt_bias), padv(a_log), padv(d_skip))


def _ssd_step_pre_kernel(xbc_ref, c0_ref, c1_ref, c2_ref, dt_ref, cw_ref, cb_ref, dtb_ref, alog_ref,
                         xc_ref, dt_out_ref, dec_ref):
    xc = (cb_ref[...] + cw_ref[0:1, :] * c0_ref[...] + cw_ref[1:2, :] * c1_ref[...]
          + cw_ref[2:3, :] * c2_ref[...] + cw_ref[3:4, :] * xbc_ref[...])
    xc_ref[...] = _silu(xc)
    dt = _softplus(dt_ref[...] + dtb_ref[...])
    dt_out_ref[...] = dt
    dec_ref[...] = jnp.exp(dt * (-jnp.exp(alog_ref[...])))


def _ssd_step_state_kernel(s_ref, x_ref, b_ref, c_ref, dt_ref, dec_ref, dskip_ref, s_out_ref, y_ref):
    tb, heads, p, n = s_ref.shape
    hpg = heads // SSD_GROUPS
    lane = lax.broadcasted_iota(jnp.int32, (p, heads), 1)
    for b in range(tb):
        x = x_ref[b]
        xdt_t = x.T * dt_ref[b]
        dec = jnp.broadcast_to(dec_ref[b], (p, heads))
        bm = b_ref[b]
        cm = c_ref[b]
        ycols = jnp.zeros((p, heads), F32)
        for h in range(heads):
            g = h // hpg
            s_new = s_ref[b, h] * dec[:, h:h + 1] + xdt_t[:, h:h + 1] * bm[g:g + 1, :]
            s_out_ref[b, h] = s_new
            yc = jnp.sum(s_new * cm[g:g + 1, :], axis=1, keepdims=True)
            ycols = jnp.where(lane == h, yc, ycols)
        y_ref[b] = ycols.T + dskip_ref[...] * x


def _ssd_step(xbc_raw, conv_state, dt_raw, state, conv_w, conv_b, dt_bias, a_log, d_skip, d_inner):
    nb, cdim = xbc_raw.shape
    hp = dt_raw.shape[-1]
    heads = d_inner // SSD_HEAD_DIM
    p, n = SSD_HEAD_DIM, SSD_STATE
    padv = lambda a: jnp.pad(a.reshape(1, heads), ((0, 0), (0, hp - heads)))
    full = lambda shape: pl.BlockSpec(shape, lambda: tuple(0 for _ in shape))
    xc, dt, dec = pl.pallas_call(
        _ssd_step_pre_kernel,
        in_specs=[full((nb, cdim))] * 4 + [full((nb, hp)), full((CONV_W, cdim)), full((1, cdim)),
                                            full((1, hp)), full((1, hp))],
        out_specs=[full((nb, cdim)), full((nb, hp)), full((nb, hp))],
        out_shape=[jax.ShapeDtypeStruct((nb, cdim), F32), jax.ShapeDtypeStruct((nb, hp), F32),
                   jax.ShapeDtypeStruct((nb, hp), F32)],
        compiler_params=pltpu.CompilerParams(vmem_limit_bytes=VMEM_LIMIT),
        name="ssd_step_pre",
    )(xbc_raw, conv_state[:, 0], conv_state[:, 1], conv_state[:, 2], dt_raw, conv_w,
      conv_b.reshape(1, cdim), padv(dt_bias), padv(a_log))
    gn = SSD_GROUPS * n
    x3 = xc[:, :d_inner].reshape(nb, heads, p)
    b3 = xc[:, d_inner:d_inner + gn].reshape(nb, SSD_GROUPS, n)
    c3 = xc[:, d_inner + gn:].reshape(nb, SSD_GROUPS, n)
    tb = 2
    s_new, y3 = pl.pallas_call(
        _ssd_step_state_kernel,
        grid=(nb // tb,),
        in_specs=[
            pl.BlockSpec((tb, heads, p, n), lambda i: (i, 0, 0, 0)),
            pl.BlockSpec((tb, heads, p), lambda i: (i, 0, 0)),
            pl.BlockSpec((tb, SSD_GROUPS, n), lambda i: (i, 0, 0)),
            pl.BlockSpec((tb, SSD_GROUPS, n), lambda i: (i, 0, 0)),
            pl.BlockSpec((tb, 1, heads), lambda i: (i, 0, 0)),
            pl.BlockSpec((tb, 1, heads), lambda i: (i, 0, 0)),
            pl.BlockSpec((heads, 1), lambda i: (0, 0)),
        ],
        out_specs=[
            pl.BlockSpec((tb, heads, p, n), lambda i: (i, 0, 0, 0)),
            pl.BlockSpec((tb, heads, p), lambda i: (i, 0, 0)),
        ],
        out_shape=[jax.ShapeDtypeStruct((nb, heads, p, n), F32), jax.ShapeDtypeStruct((nb, heads, p), F32)],
        compiler_params=_params("arbitrary"),
        name="ssd_step_state",
    )(state.reshape(nb, heads, p, n), x3, b3, c3, dt[:, :heads].reshape(nb, 1, heads),
      dec[:, :heads].reshape(nb, 1, heads), d_skip.reshape(heads, 1))
    return y3.reshape(nb, d_inner), s_new


NOT_RANKED = 99.0
PEER_PAIRS = [(a, b) for a in range(PEER_TOPK) for b in range(PEER_TOPK) if (a + 1) * (b + 1) <= PEER_TOPK]


def _peer_prep_kernel(q_ref, k1_ref, k2_ref, r2_ref, e2_ref, c1_ref, e1_ref,
                      s_scr, work_scr, rank_scr, v_scr):
    tt = q_ref.shape[0]
    hh = PEER_HEADS
    half = k1_ref.shape[1]
    tile = lambda k: slice(k * hh, (k + 1) * hh)

    for side, k_ref in enumerate((k1_ref, k2_ref)):
        s = _dot_nt(k_ref[...], q_ref[:, side * half:(side + 1) * half])
        s_scr[side] = s
        work_scr[...] = s
        rank_scr[side] = jnp.full(s.shape, NOT_RANKED, F32)

        def extract(a, _, side=side):
            w = work_scr[...]
            m = w[tile(0)]
            for k in range(1, PEER_NKEYS):
                m = jnp.maximum(m, w[tile(k)])
            sel = jnp.full((hh, tt), PEER_NKEYS, jnp.int32)
            for k in reversed(range(PEER_NKEYS)):
                sel = jnp.where(w[tile(k)] == m, k, sel)
            af = a.astype(F32)
            for k in range(PEER_NKEYS):
                hit = sel == k
                work_scr[tile(k), :] = jnp.where(hit, NEG_INF, w[tile(k)])
                rank_scr[side, tile(k), :] = jnp.where(hit, af, rank_scr[side, tile(k), :])
            v_scr[side, a] = m
            return 0

        lax.fori_loop(0, PEER_TOPK, extract, 0)

    v1 = [v_scr[0, a] for a in range(PEER_TOPK)]
    v2 = [v_scr[1, b] for b in range(PEER_TOPK)]
    cand = {c: v1[c[0]] + v2[c[1]] for c in PEER_PAIRS}
    cnt = {c: jnp.full((hh, tt), float((c[0] + 1) * (c[1] + 1) - 1), F32) for c in PEER_PAIRS}
    for n1, c in enumerate(PEER_PAIRS):
        for c2 in PEER_PAIRS[n1 + 1:]:
            if c2[0] >= c[0] and c2[1] >= c[1]:
                continue
            ahead = jnp.where(cand[c] >= cand[c2], 1.0, 0.0)
            cnt[c2] = cnt[c2] + ahead
            cnt[c] = cnt[c] + (1.0 - ahead)
    e1v = [jnp.exp(v1[a] - v1[0]) for a in range(PEER_TOPK)]
    e2v = [jnp.exp(v2[b] - v2[0]) for b in range(PEER_TOPK)]
    ncols = [jnp.zeros((hh, tt), F32) for _ in range(PEER_TOPK)]
    z = jnp.zeros((hh, tt), F32)
    for c in PEER_PAIRS:
        keep = jnp.where(cnt[c] < float(PEER_TOPK), 1.0, 0.0)
        ncols[c[0]] = ncols[c[0]] + keep
        z = z + keep * (e1v[c[0]] * e2v[c[1]])
    rz = 1.0 / z

    for k in range(PEER_NKEYS):
        r1 = rank_scr[0, tile(k), :]
        c1 = jnp.zeros((hh, tt), F32)
        for a in range(PEER_TOPK):
            c1 = jnp.where(r1 == float(a), ncols[a], c1)
        c1_ref[tile(k), :] = c1
        e1_ref[tile(k), :] = jnp.exp(s_scr[0, tile(k), :] - v1[0]) * rz
        e2_ref[tile(k), :] = jnp.exp(s_scr[1, tile(k), :] - v2[0])
        r2_ref[tile(k), :] = rank_scr[1, tile(k), :]


def _peer_prep(qp, k1big, k2big, tt=LANES):
    t = qp.shape[0]
    rows = k1big.shape[0]
    out = jax.ShapeDtypeStruct((rows, t), F32)
    ospec = pl.BlockSpec((rows, tt), lambda i: (0, i))
    return pl.pallas_call(
        _peer_prep_kernel,
        grid=(t // tt,),
        in_specs=[pl.BlockSpec((tt, qp.shape[1]), lambda i: (i, 0)),
                  pl.BlockSpec(k1big.shape, lambda i: (0, 0)),
                  pl.BlockSpec(k2big.shape, lambda i: (0, 0))],
        out_specs=[ospec] * 4,
        out_shape=[out] * 4,
        scratch_shapes=[pltpu.VMEM((2, rows, tt), F32), pltpu.VMEM((rows, tt), F32),
                        pltpu.VMEM((2, rows, tt), F32), pltpu.VMEM((2, PEER_TOPK, PEER_HEADS, tt), F32)],
        compiler_params=_params("arbitrary"),
        name="peer_prep",
    )(qp, k1big, k2big)


PEER_ROWS = 16


def _peer_expert_kernel(hc_ref, u_ref, vt_ref, r2_ref, e2_ref, c1_ref, e1_ref, x_ref, gate_ref,
                        o_ref, a_scr, p_scr, acc_scr):
    e = pl.program_id(1)
    te, tm = a_scr.shape
    nk = PEER_NKEYS

    @pl.when(e == 0)
    def _():
        acc_scr[...] = jnp.zeros(acc_scr.shape, F32)

    a_scr[...] = _dot_nt(u_ref[...], hc_ref[...])

    def lane_block(lb, _):
        lanes = pl.ds(pl.multiple_of(lb * LANES, LANES), LANES)
        for ii in range(te // nk):
            c1 = [c1_ref[h, ii:ii + 1, lanes] for h in range(PEER_HEADS)]
            e1 = [e1_ref[h, ii:ii + 1, lanes] for h in range(PEER_HEADS)]
            for j0 in range(0, nk, PEER_ROWS):
                w = jnp.zeros((PEER_ROWS, LANES), F32)
                for h in range(PEER_HEADS):
                    r2 = r2_ref[h, j0:j0 + PEER_ROWS, lanes]
                    e2 = e2_ref[h, j0:j0 + PEER_ROWS, lanes]
                    w = w + jnp.where(r2 < c1[h], e2, 0.0) * e1[h]
                rows = slice(ii * nk + j0, ii * nk + j0 + PEER_ROWS)
                p_scr[rows, lanes] = (jax.nn.gelu(a_scr[rows, lanes]) * w).astype(BF16)
        return 0

    lax.fori_loop(0, tm // LANES, lane_block, 0)
    acc_scr[...] += _dot(vt_ref[...], p_scr[...])

    @pl.when(e == pl.num_programs(1) - 1)
    def _():
        o_ref[...] = x_ref[...] + gate_ref[...] * acc_scr[...].T


def _peer_expert(hc, u_bf, vt_bf, r2, e2, c1, e1, x, gate, *, tm, te, rows_per_mod):
    t, d = hc.shape
    n_exp = u_bf.shape[0]
    hh, nk = PEER_HEADS, PEER_NKEYS
    once = pl.Buffered(1)
    if rows_per_mod == 1:
        gate_arr = gate
        gate_spec = pl.BlockSpec((tm, d), lambda i, e: (i, 0), pipeline_mode=once)
    else:
        per = rows_per_mod // tm
        gate_arr = gate.reshape(gate.shape[0], 1, d)
        gate_spec = pl.BlockSpec((None, 1, d), lambda i, e: (i // per, 0, 0))
    return pl.pallas_call(
        _peer_expert_kernel,
        grid=(t // tm, n_exp // te),
        in_specs=[
            pl.BlockSpec((tm, d), lambda i, e: (i, 0), pipeline_mode=once),
            pl.BlockSpec((te, d), lambda i, e: (e, 0)),
            pl.BlockSpec((d, te), lambda i, e: (0, e)),
            pl.BlockSpec((hh, nk, tm), lambda i, e: (0, 0, i), pipeline_mode=once),
            pl.BlockSpec((hh, nk, tm), lambda i, e: (0, 0, i), pipeline_mode=once),
            pl.BlockSpec((hh, te // nk, tm), lambda i, e: (0, e, i)),
            pl.BlockSpec((hh, te // nk, tm), lambda i, e: (0, e, i)),
            pl.BlockSpec((tm, d), lambda i, e: (i, 0), pipeline_mode=once),
            gate_spec,
        ],
        out_specs=pl.BlockSpec((tm, d), lambda i, e: (i, 0)),
        out_shape=jax.ShapeDtypeStruct((t, d), F32),
        scratch_shapes=[pltpu.VMEM((te, tm), F32), pltpu.VMEM((te, tm), BF16), pltpu.VMEM((d, tm), F32)],
        compiler_params=_params("arbitrary", "arbitrary"),
        name="peer_expert",
    )(hc, u_bf, vt_bf, r2, e2, c1, e1, x, gate_arr)


def _final_norm_kernel(x_ref, g_ref, o_ref):
    o_ref[...] = _rms(x_ref[...]) * g_ref[...]


def _final_norm(x, g, tm):
    t, d = x.shape
    return pl.pallas_call(
        _final_norm_kernel,
        grid=(t // tm,),
        in_specs=[pl.BlockSpec((tm, d), lambda i: (i, 0)), pl.BlockSpec((1, d), lambda i: (0, 0))],
        out_specs=pl.BlockSpec((tm, d), lambda i: (i, 0)),
        out_shape=jax.ShapeDtypeStruct((t, d), F32),
        compiler_params=_params("arbitrary"),
        name="final_norm",
    )(x, g.reshape(1, d))


TM_PROMPT = 512
TN_PROJ = 1024
TM_EXPERT = 512
TE_EXPERT = 1024


def _peer_layer(groups, w_q, k1, k2, u, v, norm_g):
    d = w_q.shape[0]
    hh, nk = PEER_HEADS, PEER_NKEYS
    half = k1.shape[1]
    wq = w_q.reshape(d, hh, 2, half).transpose(0, 2, 1, 3).reshape(d, 2 * hh * half).astype(BF16)
    eye = jnp.eye(hh, dtype=F32)
    kbig = lambda k: jnp.einsum("kd,hg->khgd", k, eye).reshape(nk * hh, hh * half).astype(BF16)
    k1b, k2b = kbig(k1), kbig(k2)
    u_bf = u.astype(BF16)
    vt_bf = v.T.astype(BF16)
    outs = []
    for gr in groups:
        x = gr["x"]
        t = x.shape[0]
        qp, hc = _projection("normmod", "none", (x, norm_g, gr["sc"], gr["sh"]), wq,
                             tm=gr["tm"], tn=TN_PROJ, rows_per_mod=gr["rows_per_mod"],
                             emit_lhs=True, out_dtype=BF16, name="peer_q")
        routed = _peer_prep(qp, k1b, k2b)
        r2, e2, c1, e1 = [a.reshape(nk, hh, t).transpose(1, 0, 2) for a in routed]
        outs.append(_peer_expert(hc, u_bf, vt_bf, r2, e2, c1, e1, x, gr["gate"],
                                 tm=gr["tme"], te=TE_EXPERT, rows_per_mod=gr["rows_per_mod"]))
    return outs


def kernel(x_prompt, x_sample, state_rg_conv, state_rg_h, state_ssd_conv, state_ssd, c_prompt, c_sample, norm1_g, norm2_g, w_mod, b_mod, rg_w_in, rg_b_in, rg_conv_w, rg_conv_b, rg_w_a, rg_b_a, rg_w_i, rg_b_i, rg_lambda, rg_w_out, rg_b_out, ssd_w_in, ssd_conv_w, ssd_conv_b, ssd_dt_bias, ssd_a_log, ssd_d, ssd_norm_g, ssd_w_out, peer_w_q, peer_k1, peer_k2, peer_u, peer_v, final_g):
    bp, seq, d = x_prompt.shape
    bs = x_sample.shape[0]
    tp = bp * seq
    depth = w_mod.shape[0]
    d_rnn = rg_w_out.shape[1]
    d_inner = ssd_w_out.shape[1]
    cdim = ssd_conv_w.shape[-1]
    heads = d_inner // SSD_HEAD_DIM

    nc = bp + bs
    c_all = jnp.concatenate([c_prompt, c_sample, jnp.zeros((-nc % SUBLANES, d), F32)], axis=0)
    mod = _modulation(c_all, w_mod, b_mod).reshape(depth, c_all.shape[0], 6, d)

    xp = x_prompt.reshape(tp, d)
    xs = x_sample.reshape(bs, d)
    tms = bs
    outs = {}

    for l in range(depth):
        mp = [mod[l, :bp, i] for i in range(6)]
        ms = [mod[l, bp:nc, i] for i in range(6)]
        j = l // 2
        if l % 2 == 0:
            w_in = rg_w_in[j].astype(BF16)
            w_out = rg_w_out[j].astype(BF16)
            rg = (rg_conv_w[j], rg_conv_b[j], rg_w_a[j], rg_b_a[j], rg_w_i[j], rg_b_i[j], rg_lambda[j])
            proj_p = _projection("normmod", "bias", (xp, norm1_g[l], mp[1], mp[0]), w_in, tm=TM_PROMPT,
                                 tn=TN_PROJ, rows_per_mod=seq, bias=rg_b_in[j], name="rg_in")
            proj_s = _projection("normmod", "bias", (xs, norm1_g[l], ms[1], ms[0]), w_in, tm=tms,
                                 tn=TN_PROJ, rows_per_mod=1, bias=rg_b_in[j], name="rg_in")
            proj_p3 = proj_p.reshape(bp, seq, 2 * d_rnn)
            y_p, h_p = _rglru_seq(proj_p3, *rg)
            y_s, h_s = _rglru_step(proj_s, state_rg_conv[j], state_rg_h[j], *rg)
            xp = _projection("plain", "resid", (y_p.reshape(tp, d_rnn),), w_out, tm=TM_PROMPT, tn=TN_PROJ,
                             rows_per_mod=seq, bias=rg_b_out[j], resid=(xp, mp[2]), name="rg_out")
            xs = _projection("plain", "resid", (y_s,), w_out, tm=tms, tn=TN_PROJ,
                             rows_per_mod=1, bias=rg_b_out[j], resid=(xs, ms[2]), name="rg_out")
            outs.setdefault("rg_conv_p", []).append(proj_p3[:, seq - (CONV_W - 1):, d_rnn:])
            outs.setdefault("rg_h_p", []).append(h_p.reshape(bp, d_rnn))
            outs.setdefault("rg_conv_s", []).append(
                jnp.concatenate([state_rg_conv[j][:, 1:], proj_s[:, None, d_rnn:]], axis=1))
            outs.setdefault("rg_h_s", []).append(h_s)
        else:
            w_in = ssd_w_in[j]
            w_z = w_in[:, :d_inner].astype(BF16)
            w_xbc = w_in[:, d_inner:d_inner + cdim].astype(BF16)
            w_dt = jnp.pad(w_in[:, d_inner + cdim:], ((0, 0), (0, LANES - heads))).astype(BF16)
            w_out = ssd_w_out[j].astype(BF16)
            sp = (ssd_conv_w[j], ssd_conv_b[j], ssd_dt_bias[j], ssd_a_log[j], ssd_d[j])
            z_p, hm_p, dt_p = _projection("normmod", "none", (xp, norm1_g[l], mp[1], mp[0]), w_z, tm=TM_PROMPT,
                                          tn=TN_PROJ, rows_per_mod=seq, w_extra=w_dt, emit_lhs=True, name="ssd_in_z")
            z_s, hm_s, dt_s = _projection("normmod", "none", (xs, norm1_g[l], ms[1], ms[0]), w_z, tm=tms,
                                          tn=TN_PROJ, rows_per_mod=1, w_extra=w_dt, emit_lhs=True, name="ssd_in_z")
            xbc_p = _projection("plain", "none", (hm_p,), w_xbc, tm=TM_PROMPT, tn=TN_PROJ, name="ssd_in_xbc")
            xbc_s = _projection("plain", "none", (hm_s,), w_xbc, tm=tms, tn=TN_PROJ, name="ssd_in_xbc")
            xbc_p3 = xbc_p.reshape(bp, seq, cdim)
            y_p, st_p = _ssd_seq(xbc_p3, dt_p.reshape(bp, seq, LANES), *sp, d_inner)
            y_s, st_s = _ssd_step(xbc_s, state_ssd_conv[j], dt_s, state_ssd[j], *sp, d_inner)
            xp = _projection("gated", "resid", (y_p.reshape(tp, d_inner), z_p, ssd_norm_g[j]), w_out, tm=256, tn=512,
                             rows_per_mod=seq, resid=(xp, mp[2]), name="ssd_out")
            xs = _projection("gated", "resid", (y_s, z_s, ssd_norm_g[j]), w_out, tm=tms, tn=512,
                             rows_per_mod=1, resid=(xs, ms[2]), name="ssd_out")
            g_, n_ = SSD_GROUPS, SSD_STATE
            outs.setdefault("ssd_conv_p", []).append(xbc_p3[:, seq - (CONV_W - 1):])
            outs.setdefault("ssd_p", []).append(
                st_p.reshape(bp, g_, n_, heads // g_, SSD_HEAD_DIM).transpose(0, 1, 3, 4, 2))
            outs.setdefault("ssd_conv_s", []).append(
                jnp.concatenate([state_ssd_conv[j][:, 1:], xbc_s[:, None]], axis=1))
            outs.setdefault("ssd_s", []).append(st_s.reshape(state_ssd[j].shape))

        xp, xs = _peer_layer(
            [dict(x=xp, sc=mp[4], sh=mp[3], gate=mp[5], rows_per_mod=seq, tm=TM_PROMPT, tme=TM_EXPERT),
             dict(x=xs, sc=ms[4], sh=ms[3], gate=ms[5], rows_per_mod=1, tm=tms, tme=tms)],
            peer_w_q[l], peer_k1[l], peer_k2[l], peer_u[l], peer_v[l], norm2_g[l])

    y_p = _final_norm(xp, final_g, TM_PROMPT).reshape(bp, seq, d)
    y_s = _final_norm(xs, final_g, tms).reshape(bs, 1, d)
    st = lambda name: jnp.stack(outs[name])
    return (y_p, y_s, st("rg_conv_p"), st("rg_h_p"), st("ssd_conv_p"), st("ssd_p"),
            st("rg_conv_s"), st("rg_h_s"), st("ssd_conv_s"), st("ssd_s"))
```

```python
import functools
import math

import jax
import jax.numpy as jnp
from jax import lax
from jax.experimental import pallas as pl
from jax.experimental.pallas import tpu as pltpu

F32 = jnp.float32
BF16 = jnp.bfloat16

EPS = 1e-6
CONV_W = 4
RG_BLOCKS = 8
RG_C = 8.0
SSD_HEAD_DIM = 64
SSD_GROUPS = 8
SSD_STATE = 128
SSD_CHUNK = 128
PEER_HEADS = 8
PEER_NKEYS = 128
PEER_TOPK = 16
LANES = 128
SUBLANES = 8
VMEM_LIMIT = 56 * 1024 * 1024
NEG_INF = float("-inf")


def _params(*sem):
    return pltpu.CompilerParams(dimension_semantics=sem, vmem_limit_bytes=VMEM_LIMIT)


def _silu(x):
    return x * jax.nn.sigmoid(x)


def _softplus(x):
    return jnp.maximum(x, 0.0) + jnp.log1p(jnp.exp(-jnp.abs(x)))


def _rms(x):
    return x * lax.rsqrt(jnp.mean(x * x, axis=-1, keepdims=True) + EPS)


def _dot(a, b):
    return jnp.dot(a, b, preferred_element_type=F32)


def _dot_nt(a, b):
    return lax.dot_general(a, b, (((1,), (1,)), ((), ())), preferred_element_type=F32)


def _dot_tn(a, b):
    return lax.dot_general(a, b, (((0,), (0,)), ((), ())), preferred_element_type=F32)


def _mod_kernel(c_ref, w_ref, b_ref, o_ref):
    cs = _silu(c_ref[...]).astype(BF16)
    o_ref[...] = _dot(cs, w_ref[...].astype(BF16)) + b_ref[...]


def _modulation(c_all, w_mod, b_mod, chunk):
    depth, d, n = w_mod.shape
    rows = c_all.shape[0]
    tn = 512
    per = chunk // tn
    return pl.pallas_call(
        _mod_kernel,
        grid=(depth, n // tn),
        in_specs=[
            pl.BlockSpec((rows, d), lambda l, j: (0, 0)),
            pl.BlockSpec((None, d, tn), lambda l, j: (l, 0, j)),
            pl.BlockSpec((None, 1, tn), lambda l, j: (l, 0, j)),
        ],
        out_specs=pl.BlockSpec((None, None, rows, tn), lambda l, j: (l, j // per, 0, j % per)),
        out_shape=jax.ShapeDtypeStruct((depth, n // chunk, rows, chunk), F32),
        compiler_params=_params("arbitrary", "arbitrary"),
        name="modulation",
    )(c_all, w_mod, b_mod.reshape(depth, 1, n))


def _proj_kernel(*refs, pro, epi, emit_lhs, n_extra, out_dtype):
    it = iter(refs)
    if pro == "normmod":
        x_ref, g_ref, sc_ref, sh_ref = next(it), next(it), next(it), next(it)
    elif pro == "gated":
        y_ref, z_ref, g_ref = next(it), next(it), next(it)
    else:
        x_ref = next(it)
    w_ref = next(it)
    b_ref = next(it) if epi in ("bias", "resid") else None
    if epi == "resid":
        xres_ref, gate_ref = next(it), next(it)
    wx_ref = next(it) if n_extra else None
    o_ref = next(it)
    lhs_out_ref = next(it) if emit_lhs else None
    ox_ref = next(it) if n_extra else None
    lhs_ref = next(it) if pro != "plain" else None

    j = pl.program_id(1)

    if pro != "plain":
        @pl.when(j == 0)
        def _():
            if pro == "normmod":
                x = x_ref[...]
                h = _rms(x) * g_ref[...] * (1.0 + sc_ref[...]) + sh_ref[...]
            else:
                y = y_ref[...]
                h = _rms(y * _silu(z_ref[...])) * g_ref[...]
            hb = h.astype(BF16)
            lhs_ref[...] = hb
            if emit_lhs:
                lhs_out_ref[...] = hb
            if n_extra:
                ox_ref[...] = _dot(hb, wx_ref[...])
        lhs = lhs_ref[...]
    else:
        lhs = x_ref[...]

    acc = _dot(lhs, w_ref[...])
    if b_ref is not None:
        acc = acc + b_ref[...]
    if epi == "resid":
        acc = xres_ref[...] + gate_ref[...] * acc
    o_ref[...] = acc.astype(out_dtype)


def _projection(pro, epi, ins, w, *, tm, tn, rows_per_mod=None, bias=None, resid=None,
                w_extra=None, emit_lhs=False, out_dtype=F32, name="proj"):
    k, n = w.shape
    m = ins[0].shape[0]
    assert m % tm == 0 and n % tn == 0
    grid = (m // tm, n // tn)

    def mod_spec(width_block, col_of):
        if rows_per_mod == 1:
            return pl.BlockSpec((tm, width_block), lambda i, j: (i, col_of(j)))
        assert rows_per_mod % tm == 0
        t = rows_per_mod // tm
        return pl.BlockSpec((None, 1, width_block), lambda i, j: (i // t, 0, col_of(j)))

    def mod_arr(a):
        return a if rows_per_mod == 1 else a.reshape(a.shape[0], 1, a.shape[1])

    args, specs = [], []
    if pro == "normmod":
        x, g, sc, sh = ins
        args += [x, g.reshape(1, k), mod_arr(sc), mod_arr(sh)]
        specs += [pl.BlockSpec((tm, k), lambda i, j: (i, 0)),
                  pl.BlockSpec((1, k), lambda i, j: (0, 0)),
                  mod_spec(k, lambda j: 0), mod_spec(k, lambda j: 0)]
    elif pro == "gated":
        y, z, g = ins
        args += [y, z, g.reshape(1, k)]
        specs += [pl.BlockSpec((tm, k), lambda i, j: (i, 0)),
                  pl.BlockSpec((tm, k), lambda i, j: (i, 0)),
                  pl.BlockSpec((1, k), lambda i, j: (0, 0))]
    else:
        (x,) = ins
        args += [x]
        specs += [pl.BlockSpec((tm, k), lambda i, j: (i, 0))]
    args.append(w)
    specs.append(pl.BlockSpec((k, tn), lambda i, j: (0, j)))
    if epi in ("bias", "resid"):
        b = bias if bias is not None else jnp.zeros((n,), F32)
        args.append(b.reshape(1, n))
        specs.append(pl.BlockSpec((1, tn), lambda i, j: (0, j)))
    if epi == "resid":
        xres, gate = resid
        args += [xres, mod_arr(gate)]
        specs += [pl.BlockSpec((tm, tn), lambda i, j: (i, j)), mod_spec(tn, lambda j: j)]
    n_extra = 0
    if w_extra is not None:
        n_extra = w_extra.shape[1]
        args.append(w_extra)
        specs.append(pl.BlockSpec((k, n_extra), lambda i, j: (0, 0)))

    out_shapes = [jax.ShapeDtypeStruct((m, n), out_dtype)]
    out_specs = [pl.BlockSpec((tm, tn), lambda i, j: (i, j))]
    if emit_lhs:
        out_shapes.append(jax.ShapeDtypeStruct((m, k), BF16))
        out_specs.append(pl.BlockSpec((tm, k), lambda i, j: (i, 0)))
    if n_extra:
        out_shapes.append(jax.ShapeDtypeStruct((m, n_extra), F32))
        out_specs.append(pl.BlockSpec((tm, n_extra), lambda i, j: (i, 0)))
    scratch = [pltpu.VMEM((tm, k), BF16)] if pro != "plain" else []

    outs = pl.pallas_call(
        functools.partial(_proj_kernel, pro=pro, epi=epi, emit_lhs=emit_lhs, n_extra=n_extra,
                          out_dtype=out_dtype),
        grid=grid, in_specs=specs, out_specs=out_specs, out_shape=out_shapes,
        scratch_shapes=scratch,
        compiler_params=_params("arbitrary", "arbitrary"),
        name=name,
    )(*args)
    return outs if len(outs) > 1 else outs[0]


def _rglru_gates(xc, wa_ref, ba_ref, wi_ref, bi_ref, lam_ref):
    xcb = xc.astype(BF16)
    r = jax.nn.sigmoid(_dot(xcb, wa_ref[...]) + ba_ref[...])
    i = jax.nn.sigmoid(_dot(xcb, wi_ref[...]) + bi_ref[...])
    log_a = (-RG_C * _softplus(-lam_ref[...])) * r
    a = jnp.exp(log_a)
    mult = jnp.sqrt(jnp.maximum(-jnp.tanh(log_a) * (a * a + 1.0), 0.0))
    return a, mult * (i * xc)


RG_ROWS = 256


def _rglru_seq_kernel(gate_ref, xr_ref, cw_ref, cb_ref, wa_ref, ba_ref, wi_ref, bi_ref, lam_ref,
                      y_ref, hlast_ref, xpad_ref):
    seq, blk = xr_ref.shape
    pad = SUBLANES
    xpad_ref[0:pad, :] = jnp.zeros((pad, blk), F32)
    xpad_ref[pad:pad + seq, :] = xr_ref[...]
    row_in_tile = lax.broadcasted_iota(jnp.int32, (RG_ROWS, blk), 0) % SUBLANES

    def step(c, carry):
        start = pl.multiple_of(c * RG_ROWS, RG_ROWS)
        xe = xpad_ref[pl.ds(start, RG_ROWS + pad), :]
        xc = cb_ref[...] + cw_ref[3:4, :] * xe[pad:]
        for k in range(1, CONV_W):
            xc = xc + cw_ref[3 - k:4 - k, :] * pltpu.roll(xe, k, axis=0)[pad:]
        a, b = _rglru_gates(xc, wa_ref, ba_ref, wi_ref, bi_ref, lam_ref)
        for s in (1, 2, 4):
            m = row_in_tile >= s
            b = jnp.where(m, a * pltpu.roll(b, s, axis=0) + b, b)
            a = jnp.where(m, a * pltpu.roll(a, s, axis=0), a)
        hs = []
        for t in range(RG_ROWS // SUBLANES):
            sl = slice(t * SUBLANES, (t + 1) * SUBLANES)
            h = b[sl] + a[sl] * carry
            hs.append(h)
            carry = h[SUBLANES - 1:SUBLANES, :]
        h_all = jnp.concatenate(hs, axis=0)
        y_ref[pl.ds(start, RG_ROWS), :] = (h_all * jax.nn.gelu(gate_ref[pl.ds(start, RG_ROWS), :])).astype(BF16)
        return carry

    carry = lax.fori_loop(0, seq // RG_ROWS, step, jnp.zeros((1, blk), F32))
    hlast_ref[...] = carry


def _rglru_seq(proj, conv_w, conv_b, w_a, b_a, w_i, b_i, lam):
    nb, seq, two_d = proj.shape
    d = two_d // 2
    blk = d // RG_BLOCKS
    vec = lambda a: a.reshape(1, d)
    vspec = pl.BlockSpec((1, blk), lambda b, k: (0, k))
    wspec = pl.BlockSpec((None, blk, blk), lambda b, k: (k, 0, 0))
    return pl.pallas_call(
        _rglru_seq_kernel,
        grid=(nb, RG_BLOCKS),
        in_specs=[
            pl.BlockSpec((None, seq, blk), lambda b, k: (b, 0, k)),
            pl.BlockSpec((None, seq, blk), lambda b, k: (b, 0, RG_BLOCKS + k)),
            pl.BlockSpec((CONV_W, blk), lambda b, k: (0, k)),
            vspec, wspec, vspec, wspec, vspec, vspec,
        ],
        out_specs=[
            pl.BlockSpec((None, seq, blk), lambda b, k: (b, 0, k)),
            pl.BlockSpec((None, 1, blk), lambda b, k: (b, 0, k)),
        ],
        out_shape=[jax.ShapeDtypeStruct((nb, seq, d), BF16), jax.ShapeDtypeStruct((nb, 1, d), F32)],
        scratch_shapes=[pltpu.VMEM((seq + SUBLANES, blk), F32)],
        compiler_params=_params("arbitrary", "arbitrary"),
        name="rglru_seq",
    )(proj, proj, conv_w, vec(conv_b), w_a.astype(BF16), vec(b_a), w_i.astype(BF16), vec(b_i), vec(lam))


def _rglru_step_kernel(gate_ref, xr_ref, c0_ref, c1_ref, c2_ref, h0_ref, cw_ref, cb_ref,
                       wa_ref, ba_ref, wi_ref, bi_ref, lam_ref, y_ref, h_ref):
    xc = (cb_ref[...] + cw_ref[0:1, :] * c0_ref[...] + cw_ref[1:2, :] * c1_ref[...]
          + cw_ref[2:3, :] * c2_ref[...] + cw_ref[3:4, :] * xr_ref[...])
    a, b = _rglru_gates(xc, wa_ref, ba_ref, wi_ref, bi_ref, lam_ref)
    h = a * h0_ref[...] + b
    h_ref[...] = h
    y_ref[...] = (h * jax.nn.gelu(gate_ref[...])).astype(BF16)


def _rglru_step(proj, conv_state, h0, conv_w, conv_b, w_a, b_a, w_i, b_i, lam):
    nb, two_d = proj.shape
    d = two_d // 2
    blk = d // RG_BLOCKS
    vec = lambda a: a.reshape(1, d)
    rows = pl.BlockSpec((nb, blk), lambda k: (0, k))
    vspec = pl.BlockSpec((1, blk), lambda k: (0, k))
    wspec = pl.BlockSpec((None, blk, blk), lambda k: (k, 0, 0))
    return pl.pallas_call(
        _rglru_step_kernel,
        grid=(RG_BLOCKS,),
        in_specs=[rows, pl.BlockSpec((nb, blk), lambda k: (0, RG_BLOCKS + k)), rows, rows, rows, rows,
                  pl.BlockSpec((CONV_W, blk), lambda k: (0, k)),
                  vspec, wspec, vspec, wspec, vspec, vspec],
        out_specs=[rows, rows],
        out_shape=[jax.ShapeDtypeStruct((nb, d), BF16), jax.ShapeDtypeStruct((nb, d), F32)],
        compiler_params=_params("arbitrary"),
        name="rglru_step",
    )(proj, proj, conv_state[:, 0], conv_state[:, 1], conv_state[:, 2], h0, conv_w, vec(conv_b),
      w_a.astype(BF16), vec(b_a), w_i.astype(BF16), vec(b_i), vec(lam))


def _split3(x):
    hi = x.astype(BF16)
    r1 = x - hi.astype(F32)
    mid = r1.astype(BF16)
    lo = (r1 - mid.astype(F32)).astype(BF16)
    return hi, mid, lo


def _ssd_seq_kernel(xbc_ref, halo_ref, dt_ref, cw_ref, cb_ref, dtb_ref, alog_ref, dskip_ref,
                    y_ref, st_ref, *, d_inner):
    c = pl.program_id(1)
    q = SSD_CHUNK
    p = SSD_HEAD_DIM
    n = SSD_STATE
    hpg = d_inner // p // SSD_GROUPS
    pad = SUBLANES

    @pl.when(c == 0)
    def _():
        st_ref[...] = jnp.zeros(st_ref.shape, F32)

    halo = jnp.where(c > 0, halo_ref[...], 0.0)
    xe = jnp.concatenate([halo, xbc_ref[...]], axis=0)
    xc = cb_ref[...] + cw_ref[3:4, :] * xe[pad:]
    for k in range(1, CONV_W):
        xc = xc + cw_ref[3 - k:4 - k, :] * pltpu.roll(xe, k, axis=0)[pad:]
    xc = _silu(xc)

    dt = _softplus(dt_ref[...] + dtb_ref[...])
    da = dt * (-jnp.exp(alog_ref[...]))
    ri = lax.broadcasted_iota(jnp.int32, (q, q), 0)
    ci = lax.broadcasted_iota(jnp.int32, (q, q), 1)
    tri = ri >= ci
    ltri = jnp.where(tri, 1.0, 0.0).astype(BF16)
    hi, mid, lo = _split3(da)
    cum = _dot(ltri, hi) + _dot(ltri, mid) + _dot(ltri, lo)
    cum_t = cum.T
    dt_t = dt.T
    last = cum[q - 1:q, :]
    ecum = jnp.exp(cum)
    wend = jnp.exp(last - cum) * dt
    elast = jnp.exp(last)
    lane = lax.broadcasted_iota(jnp.int32, (1, hpg * p), 1) // p

    for g in range(SSD_GROUPS):
        bb = xc[:, d_inner + g * n:d_inner + (g + 1) * n].astype(BF16)
        cc = xc[:, d_inner + SSD_GROUPS * n + g * n:d_inner + SSD_GROUPS * n + (g + 1) * n].astype(BF16)
        cb = _dot_nt(cc, bb)
        st = st_ref[g]
        ys = _dot(cc, st.astype(BF16))
        xw_parts = []
        elrow = jnp.zeros((1, hpg * p), F32)
        y_parts = []
        for e in range(hpg):
            h = g * hpg + e
            col = cum[:, h:h + 1]
            decay = jnp.exp(jnp.where(tri, col - cum_t[h:h + 1, :], NEG_INF))
            w = decay * cb * dt_t[h:h + 1, :]
            xh = xc[:, h * p:(h + 1) * p]
            yh = (_dot(w.astype(BF16), xh.astype(BF16)) + ys[:, e * p:(e + 1) * p] * ecum[:, h:h + 1]
                  + dskip_ref[:, h:h + 1] * xh)
            y_parts.append(yh)
            xw_parts.append(xh * wend[:, h:h + 1])
            elrow = jnp.where(lane == e, elast[:, h:h + 1], elrow)
        y_ref[:, g * hpg * p:(g + 1) * hpg * p] = jnp.concatenate(y_parts, axis=1)
        xw = jnp.concatenate(xw_parts, axis=1).astype(BF16)
        st_ref[g] = st * elrow + _dot_tn(bb, xw)


def _ssd_seq(xbc, dt_raw, conv_w, conv_b, dt_bias, a_log, d_skip, d_inner):
    nb, seq, cdim = xbc.shape
    q = SSD_CHUNK
    hp = dt_raw.shape[-1]
    heads = d_inner // SSD_HEAD_DIM
    epp = d_inner // SSD_GROUPS
    padv = lambda a: jnp.pad(a.reshape(1, heads), ((0, 0), (0, hp - heads)))
    halo_blocks = q // SUBLANES
    return pl.pallas_call(
        functools.partial(_ssd_seq_kernel, d_inner=d_inner),
        grid=(nb, seq // q),
        in_specs=[
            pl.BlockSpec((None, q, cdim), lambda b, c: (b, c, 0)),
            pl.BlockSpec((None, SUBLANES, cdim), lambda b, c: (b, jnp.maximum(c * halo_blocks - 1, 0), 0)),
            pl.BlockSpec((None, q, hp), lambda b, c: (b, c, 0)),
            pl.BlockSpec((CONV_W, cdim), lambda b, c: (0, 0)),
            pl.BlockSpec((1, cdim), lambda b, c: (0, 0)),
            pl.BlockSpec((1, hp), lambda b, c: (0, 0)),
            pl.BlockSpec((1, hp), lambda b, c: (0, 0)),
            pl.BlockSpec((1, hp), lambda b, c: (0, 0)),
        ],
        out_specs=[
            pl.BlockSpec((None, q, d_inner), lambda b, c: (b, c, 0)),
            pl.BlockSpec((None, SSD_GROUPS, SSD_STATE, epp), lambda b, c: (b, 0, 0, 0)),
        ],
        out_shape=[jax.ShapeDtypeStruct((nb, seq, d_inner), F32),
                   jax.ShapeDtypeStruct((nb, SSD_GROUPS, SSD_STATE, epp), F32)],
        compiler_params=_params("arbitrary", "arbitrary"),
        name="ssd_seq",
    )(xbc, xbc, dt_raw, conv_w, conv_b.reshape(1, cdim), padv(dt_bias), padv(a_log), padv(d_skip))


def _ssd_step_pre_kernel(xbc_ref, c0_ref, c1_ref, c2_ref, dt_ref, cw_ref, cb_ref, dtb_ref, alog_ref, drep_ref,
                         xdt_ref, skip_ref, bc_ref, dec_ref, *, d_inner):
    xc = (cb_ref[...] + cw_ref[0:1, :] * c0_ref[...] + cw_ref[1:2, :] * c1_ref[...]
          + cw_ref[2:3, :] * c2_ref[...] + cw_ref[3:4, :] * xbc_ref[...])
    xc = _silu(xc)
    dt = _softplus(dt_ref[...] + dtb_ref[...])
    dec_ref[...] = jnp.exp(dt * (-jnp.exp(alog_ref[...])))
    hp = dt.shape[1]
    head_of_lane = lax.broadcasted_iota(jnp.int32, (hp, d_inner), 1) // SSD_HEAD_DIM
    expand = jnp.where(head_of_lane == lax.broadcasted_iota(jnp.int32, (hp, d_inner), 0), 1.0, 0.0).astype(BF16)
    dt_rep = sum(_dot(piece, expand) for piece in _split3(dt))
    xs = xc[:, :d_inner]
    xdt_ref[...] = xs * dt_rep
    skip_ref[...] = xs * drep_ref[...]
    bc_ref[...] = xc[:, d_inner:]


def _ssd_step_state_kernel(dec_ref, s_ref, xdt_ref, skip_ref, b_ref, c_ref, s_out_ref, y_ref):
    tb, npair, rows, n = s_ref.shape
    pairs_per_group = npair // SSD_GROUPS
    seq0 = pl.program_id(0) * tb
    row_id = lax.broadcasted_iota(jnp.int32, (rows, n), 0)
    pair_id = lax.broadcasted_iota(jnp.int32, (npair, n), 0)
    for b in range(tb):
        xt = xdt_ref[b].T.astype(BF16)
        bm = b_ref[b]
        brep = jnp.concatenate([jnp.broadcast_to(bm[g:g + 1], (pairs_per_group, n))
                                for g in range(SSD_GROUPS)], axis=0)
        bbig = jnp.concatenate([jnp.where(pair_id == j, brep, 0.0) for j in range(npair)], axis=1)
        upd = _dot(xt, bbig.astype(BF16))
        cm = c_ref[b].astype(BF16)
        for j in range(npair):
            g = j // pairs_per_group
            d0 = dec_ref[seq0 + b, 2 * j]
            d1 = dec_ref[seq0 + b, 2 * j + 1]
            s_new = s_ref[b, j] * jnp.where(row_id < rows // 2, d0, d1) + upd[:, j * n:(j + 1) * n]
            s_out_ref[b, j] = s_new
            y8 = _dot_nt(cm, s_new.astype(BF16))
            y_ref[b, j:j + 1, :] = y8[g:g + 1, :] + skip_ref[b, j:j + 1, :]


def _ssd_step(xbc_raw, conv_state, dt_raw, state, conv_w, conv_b, dt_bias, a_log, d_skip, d_inner):
    nb, cdim = xbc_raw.shape
    hp = dt_raw.shape[-1]
    heads = d_inner // SSD_HEAD_DIM
    p, n = SSD_HEAD_DIM, SSD_STATE
    gn = SSD_GROUPS * n
    padv = lambda a: jnp.pad(a.reshape(1, heads), ((0, 0), (0, hp - heads)))
    full = lambda shape: pl.BlockSpec(shape, lambda: tuple(0 for _ in shape))
    xdt, skip, bc, dec = pl.pallas_call(
        functools.partial(_ssd_step_pre_kernel, d_inner=d_inner),
        in_specs=[full((nb, cdim))] * 4 + [full((nb, hp)), full((CONV_W, cdim)), full((1, cdim)),
                                            full((1, hp)), full((1, hp)), full((1, d_inner))],
        out_specs=[full((nb, d_inner)), full((nb, d_inner)), full((nb, 2 * gn)), full((nb, hp))],
        out_shape=[jax.ShapeDtypeStruct((nb, d_inner), F32), jax.ShapeDtypeStruct((nb, d_inner), F32),
                   jax.ShapeDtypeStruct((nb, 2 * gn), F32), jax.ShapeDtypeStruct((nb, hp), F32)],
        compiler_params=pltpu.CompilerParams(vmem_limit_bytes=VMEM_LIMIT),
        name="ssd_step_pre",
    )(xbc_raw, conv_state[:, 0], conv_state[:, 1], conv_state[:, 2], dt_raw, conv_w,
      conv_b.reshape(1, cdim), padv(dt_bias), padv(a_log), jnp.repeat(d_skip, p).reshape(1, d_inner))
    npair = heads // 2
    pair3 = lambda a: a.reshape(nb, npair, 2 * p)
    tb = 2
    sspec = pl.BlockSpec((tb, npair, 2 * p, n), lambda i: (i, 0, 0, 0))
    vspec = pl.BlockSpec((tb, npair, 2 * p), lambda i: (i, 0, 0))
    gspec = pl.BlockSpec((tb, SSD_GROUPS, n), lambda i: (i, 0, 0))
    s_new, y3 = pl.pallas_call(
        _ssd_step_state_kernel,
        grid=(nb // tb,),
        in_specs=[pl.BlockSpec(memory_space=pltpu.SMEM), sspec, vspec, vspec, gspec, gspec],
        out_specs=[sspec, vspec],
        out_shape=[jax.ShapeDtypeStruct((nb, npair, 2 * p, n), F32),
                   jax.ShapeDtypeStruct((nb, npair, 2 * p), F32)],
        compiler_params=_params("arbitrary"),
        name="ssd_step_state",
    )(dec, state.reshape(nb, npair, 2 * p, n), pair3(xdt), pair3(skip),
      bc[:, :gn].reshape(nb, SSD_GROUPS, n), bc[:, gn:].reshape(nb, SSD_GROUPS, n))
    return y3.reshape(nb, d_inner), s_new


NOT_RANKED = 99.0


def _tree(op, xs):
    xs = list(xs)
    while len(xs) > 1:
        xs = [op(xs[i], xs[i + 1]) if i + 1 < len(xs) else xs[i] for i in range(0, len(xs), 2)]
    return xs[0]
PEER_PAIRS = [(a, b) for a in range(PEER_TOPK) for b in range(PEER_TOPK) if (a + 1) * (b + 1) <= PEER_TOPK]


def _peer_prep_kernel(q_ref, k1_ref, k2_ref, r2_ref, e2_ref, c1_ref, e1_ref,
                      s_scr, work_scr, rank_scr, v_scr):
    tt = q_ref.shape[0]
    hh = PEER_HEADS
    half = k1_ref.shape[1]
    tile = lambda k: slice(k * hh, (k + 1) * hh)

    for side, k_ref in enumerate((k1_ref, k2_ref)):
        s = _dot_nt(k_ref[...], q_ref[:, side * half:(side + 1) * half])
        s_scr[side] = s
        work_scr[side] = s
        rank_scr[side] = jnp.full(s.shape, NOT_RANKED, F32)

    def extract(a, _):
        af = a.astype(F32)
        for side in range(2):
            w = [work_scr[side, tile(k), :] for k in range(PEER_NKEYS)]
            m = _tree(jnp.maximum, w)
            sel = _tree(jnp.minimum, [jnp.where(w[k] == m, k, PEER_NKEYS) for k in range(PEER_NKEYS)])
            for k in range(PEER_NKEYS):
                hit = sel == k
                work_scr[side, tile(k), :] = jnp.where(hit, NEG_INF, w[k])
                rank_scr[side, tile(k), :] = jnp.where(hit, af, rank_scr[side, tile(k), :])
            v_scr[side, a] = m
        return 0

    lax.fori_loop(0, PEER_TOPK, extract, 0)

    v1 = [v_scr[0, a] for a in range(PEER_TOPK)]
    v2 = [v_scr[1, b] for b in range(PEER_TOPK)]
    cand = {c: v1[c[0]] + v2[c[1]] for c in PEER_PAIRS}
    ahead_of = {c: [] for c in PEER_PAIRS}
    behind_of = {c: [] for c in PEER_PAIRS}
    for n1, c in enumerate(PEER_PAIRS):
        for c2 in PEER_PAIRS[n1 + 1:]:
            if c2[0] >= c[0] and c2[1] >= c[1]:
                continue
            ahead = jnp.where(cand[c] >= cand[c2], 1.0, 0.0)
            ahead_of[c2].append(ahead)
            behind_of[c].append(ahead)
    e1v = [jnp.exp(v1[a] - v1[0]) for a in range(PEER_TOPK)]
    e2v = [jnp.exp(v2[b] - v2[0]) for b in range(PEER_TOPK)]
    keep_terms = [[] for _ in range(PEER_TOPK)]
    z_terms = []
    for c in PEER_PAIRS:
        cnt = jnp.full((hh, tt), float((c[0] + 1) * (c[1] + 1) - 1 + len(behind_of[c])), F32)
        if ahead_of[c]:
            cnt = cnt + _tree(jnp.add, ahead_of[c])
        if behind_of[c]:
            cnt = cnt - _tree(jnp.add, behind_of[c])
        keep = jnp.where(cnt < float(PEER_TOPK), 1.0, 0.0)
        keep_terms[c[0]].append(keep)
        z_terms.append(keep * (e1v[c[0]] * e2v[c[1]]))
    ncols = [_tree(jnp.add, t) for t in keep_terms]
    z = _tree(jnp.add, z_terms)
    rz = 1.0 / z

    twice = lambda a: jnp.concatenate([a, a], axis=0)
    ncols2 = [twice(c) for c in ncols]
    m1, m2, rz2 = twice(v1[0]), twice(v2[0]), twice(rz)
    for k in range(0, PEER_NKEYS, 2):
        rows = slice(k * hh, (k + 2) * hh)
        r1 = rank_scr[0, rows, :]
        c1 = jnp.zeros((2 * hh, tt), F32)
        for a in range(PEER_TOPK):
            c1 = jnp.where(r1 == float(a), ncols2[a], c1)
        c1_ref[rows, :] = c1
        e1_ref[rows, :] = jnp.exp(s_scr[0, rows, :] - m1) * rz2
        e2_ref[rows, :] = jnp.exp(s_scr[1, rows, :] - m2).astype(BF16)
        r2_ref[rows, :] = rank_scr[1, rows, :].astype(BF16)


def _peer_prep(qp, k1big, k2big, tt=LANES):
    t = qp.shape[0]
    rows = k1big.shape[0]
    out = lambda dt: jax.ShapeDtypeStruct((rows, t), dt)
    ospec = pl.BlockSpec((rows, tt), lambda i: (0, i))
    return pl.pallas_call(
        _peer_prep_kernel,
        grid=(t // tt,),
        in_specs=[pl.BlockSpec((tt, qp.shape[1]), lambda i: (i, 0)),
                  pl.BlockSpec(k1big.shape, lambda i: (0, 0)),
                  pl.BlockSpec(k2big.shape, lambda i: (0, 0))],
        out_specs=[ospec] * 4,
        out_shape=[out(BF16), out(BF16), out(F32), out(F32)],
        scratch_shapes=[pltpu.VMEM((2, rows, tt), F32), pltpu.VMEM((2, rows, tt), F32),
                        pltpu.VMEM((2, rows, tt), F32), pltpu.VMEM((2, PEER_TOPK, PEER_HEADS, tt), F32)],
        compiler_params=_params("arbitrary"),
        name="peer_prep",
    )(qp, k1big, k2big)


def _peer_expert_kernel(hc_ref, u_ref, vt_ref, r2_ref, e2_ref, c1_ref, e1_ref, x_ref, gate_ref,
                        o_ref, acc_scr):
    e = pl.program_id(1)
    te = u_ref.shape[0]
    nk = PEER_NKEYS

    @pl.when(e == 0)
    def _():
        acc_scr[...] = jnp.zeros(acc_scr.shape, F32)

    act = jax.nn.gelu(_dot_nt(u_ref[...], hc_ref[...])).astype(BF16)
    blocks = []
    for ii in range(te // nk):
        w = None
        for h in range(PEER_HEADS):
            row = ii * PEER_HEADS + h
            c1 = c1_ref[row:row + 1, :].astype(BF16)
            e1 = e1_ref[row:row + 1, :].astype(BF16)
            term = jnp.where(r2_ref[h] < c1, e2_ref[h], jnp.zeros((), BF16)) * e1
            w = term if w is None else w + term
        blocks.append(act[ii * nk:(ii + 1) * nk] * w)
    acc_scr[...] += _dot(vt_ref[...], jnp.concatenate(blocks, axis=0))

    @pl.when(e == pl.num_programs(1) - 1)
    def _():
        o_ref[...] = x_ref[...] + gate_ref[...] * acc_scr[...].T


def _peer_expert(hc, u_bf, vt_bf, layer, r2, e2, c1, e1, x, gate, *, tm, te, rows_per_mod):
    t, d = hc.shape
    n_exp = u_bf.shape[1]
    rows_i = te // PEER_NKEYS * PEER_HEADS
    hh, nk = PEER_HEADS, PEER_NKEYS
    once = pl.Buffered(1)
    if rows_per_mod == 1:
        gate_arr = gate
        gate_spec = pl.BlockSpec((tm, d), lambda i, e: (i, 0), pipeline_mode=once)
    else:
        per = rows_per_mod // tm
        gate_arr = gate.reshape(gate.shape[0], 1, d)
        gate_spec = pl.BlockSpec((None, 1, d), lambda i, e: (i // per, 0, 0))
    return pl.pallas_call(
        _peer_expert_kernel,
        grid=(t // tm, n_exp // te),
        in_specs=[
            pl.BlockSpec((tm, d), lambda i, e: (i, 0), pipeline_mode=once),
            pl.BlockSpec((None, te, d), lambda i, e: (layer, e, 0)),
            pl.BlockSpec((None, d, te), lambda i, e: (layer, 0, e)),
            pl.BlockSpec((hh, nk, tm), lambda i, e: (0, 0, i), pipeline_mode=once),
            pl.BlockSpec((hh, nk, tm), lambda i, e: (0, 0, i), pipeline_mode=once),
            pl.BlockSpec((rows_i, tm), lambda i, e: (e, i)),
            pl.BlockSpec((rows_i, tm), lambda i, e: (e, i)),
            pl.BlockSpec((tm, d), lambda i, e: (i, 0), pipeline_mode=once),
            gate_spec,
        ],
        out_specs=pl.BlockSpec((tm, d), lambda i, e: (i, 0)),
        out_shape=jax.ShapeDtypeStruct((t, d), F32),
        scratch_shapes=[pltpu.VMEM((d, tm), F32)],
        compiler_params=_params("arbitrary", "arbitrary"),
        name="peer_expert",
    )(hc, u_bf, vt_bf, r2, e2, c1, e1, x, gate_arr)


def _final_norm_kernel(x_ref, g_ref, o_ref):
    o_ref[...] = _rms(x_ref[...]) * g_ref[...]


def _final_norm(x, g, tm):
    t, d = x.shape
    return pl.pallas_call(
        _final_norm_kernel,
        grid=(t // tm,),
        in_specs=[pl.BlockSpec((tm, d), lambda i: (i, 0)), pl.BlockSpec((1, d), lambda i: (0, 0))],
        out_specs=pl.BlockSpec((tm, d), lambda i: (i, 0)),
        out_shape=jax.ShapeDtypeStruct((t, d), F32),
        compiler_params=_params("arbitrary"),
        name="final_norm",
    )(x, g.reshape(1, d))


TM_PROMPT = 512
TN_PROJ = 1024
TM_EXPERT = 512
TE_EXPERT = 1024


def _peer_layer(groups, w_q, k1, k2, u_bf, vt_bf, layer, norm_g):
    d = w_q.shape[0]
    hh, nk = PEER_HEADS, PEER_NKEYS
    half = k1.shape[1]
    wq = w_q.reshape(d, hh, 2, half).transpose(0, 2, 1, 3).reshape(d, 2 * hh * half).astype(BF16)
    eye = jnp.eye(hh, dtype=F32)
    kbig = lambda k: jnp.einsum("kd,hg->khgd", k, eye).reshape(nk * hh, hh * half).astype(BF16)
    k1b, k2b = kbig(k1), kbig(k2)
    outs = []
    for gr in groups:
        x = gr["x"]
        t = x.shape[0]
        qp, hc = _projection("normmod", "none", (x, norm_g, gr["sc"], gr["sh"]), wq,
                             tm=gr["tm"], tn=TN_PROJ, rows_per_mod=gr["rows_per_mod"],
                             emit_lhs=True, out_dtype=BF16, name="peer_q")
        r2, e2, c1, e1 = _peer_prep(qp, k1b, k2b)
        r2, e2 = [a.reshape(nk, hh, t).transpose(1, 0, 2) for a in (r2, e2)]
        outs.append(_peer_expert(hc, u_bf, vt_bf, layer, r2, e2, c1, e1, x, gr["gate"],
                                 tm=gr["tme"], te=TE_EXPERT, rows_per_mod=gr["rows_per_mod"]))
    return outs


def kernel(x_prompt, x_sample, state_rg_conv, state_rg_h, state_ssd_conv, state_ssd, c_prompt, c_sample, norm1_g, norm2_g, w_mod, b_mod, rg_w_in, rg_b_in, rg_conv_w, rg_conv_b, rg_w_a, rg_b_a, rg_w_i, rg_b_i, rg_lambda, rg_w_out, rg_b_out, ssd_w_in, ssd_conv_w, ssd_conv_b, ssd_dt_bias, ssd_a_log, ssd_d, ssd_norm_g, ssd_w_out, peer_w_q, peer_k1, peer_k2, peer_u, peer_v, final_g):
    bp, seq, d = x_prompt.shape
    bs = x_sample.shape[0]
    tp = bp * seq
    depth = w_mod.shape[0]
    d_rnn = rg_w_out.shape[1]
    d_inner = ssd_w_out.shape[1]
    cdim = ssd_conv_w.shape[-1]
    heads = d_inner // SSD_HEAD_DIM

    nc = bp + bs
    c_all = jnp.concatenate([c_sample, c_prompt, jnp.zeros((-nc % SUBLANES, d), F32)], axis=0)
    mod = _modulation(c_all, w_mod, b_mod, d)
    u_bf = peer_u.astype(BF16)
    vt_bf = jnp.swapaxes(peer_v, 1, 2).astype(BF16)

    xp = x_prompt.reshape(tp, d)
    xs = x_sample.reshape(bs, d)
    tms = bs
    outs = {}

    for l in range(depth):
        mp = [mod[l, i, bs:nc] for i in range(6)]
        ms = [mod[l, i, :bs] for i in range(6)]
        j = l // 2
        if l % 2 == 0:
            w_in = rg_w_in[j].astype(BF16)
            w_out = rg_w_out[j].astype(BF16)
            rg = (rg_conv_w[j], rg_conv_b[j], rg_w_a[j], rg_b_a[j], rg_w_i[j], rg_b_i[j], rg_lambda[j])
            proj_p = _projection("normmod", "bias", (xp, norm1_g[l], mp[1], mp[0]), w_in, tm=TM_PROMPT,
                                 tn=TN_PROJ, rows_per_mod=seq, bias=rg_b_in[j], name="rg_in")
            proj_s = _projection("normmod", "bias", (xs, norm1_g[l], ms[1], ms[0]), w_in, tm=tms,
                                 tn=TN_PROJ, rows_per_mod=1, bias=rg_b_in[j], name="rg_in")
            proj_p3 = proj_p.reshape(bp, seq, 2 * d_rnn)
            y_p, h_p = _rglru_seq(proj_p3, *rg)
            y_s, h_s = _rglru_step(proj_s, state_rg_conv[j], state_rg_h[j], *rg)
            xp = _projection("plain", "resid", (y_p.reshape(tp, d_rnn),), w_out, tm=TM_PROMPT, tn=TN_PROJ,
                             rows_per_mod=seq, bias=rg_b_out[j], resid=(xp, mp[2]), name="rg_out")
            xs = _projection("plain", "resid", (y_s,), w_out, tm=tms, tn=TN_PROJ,
                             rows_per_mod=1, bias=rg_b_out[j], resid=(xs, ms[2]), name="rg_out")
            outs.setdefault("rg_conv_p", []).append(proj_p3[:, seq - (CONV_W - 1):, d_rnn:])
            outs.setdefault("rg_h_p", []).append(h_p.reshape(bp, d_rnn))
            outs.setdefault("rg_conv_s", []).append(
                jnp.concatenate([state_rg_conv[j][:, 1:], proj_s[:, None, d_rnn:]], axis=1))
            outs.setdefault("rg_h_s", []).append(h_s)
        else:
            w_in = ssd_w_in[j]
            w_z = w_in[:, :d_inner].astype(BF16)
            w_xbc = w_in[:, d_inner:d_inner + cdim].astype(BF16)
            w_dt = jnp.pad(w_in[:, d_inner + cdim:], ((0, 0), (0, LANES - heads))).astype(BF16)
            w_out = ssd_w_out[j].astype(BF16)
            sp = (ssd_conv_w[j], ssd_conv_b[j], ssd_dt_bias[j], ssd_a_log[j], ssd_d[j])
            z_p, hm_p, dt_p = _projection("normmod", "none", (xp, norm1_g[l], mp[1], mp[0]), w_z, tm=TM_PROMPT,
                                          tn=TN_PROJ, rows_per_mod=seq, w_extra=w_dt, emit_lhs=True, name="ssd_in_z")
            z_s, hm_s, dt_s = _projection("normmod", "none", (xs, norm1_g[l], ms[1], ms[0]), w_z, tm=tms,
                                          tn=TN_PROJ, rows_per_mod=1, w_extra=w_dt, emit_lhs=True, name="ssd_in_z")
            xbc_p = _projection("plain", "none", (hm_p,), w_xbc, tm=TM_PROMPT, tn=TN_PROJ, name="ssd_in_xbc")
            xbc_s = _projection("plain", "none", (hm_s,), w_xbc, tm=tms, tn=TN_PROJ, name="ssd_in_xbc")
            xbc_p3 = xbc_p.reshape(bp, seq, cdim)
            y_p, st_p = _ssd_seq(xbc_p3, dt_p.reshape(bp, seq, LANES), *sp, d_inner)
            y_s, st_s = _ssd_step(xbc_s, state_ssd_conv[j], dt_s, state_ssd[j], *sp, d_inner)
            xp = _projection("gated", "resid", (y_p.reshape(tp, d_inner), z_p, ssd_norm_g[j]), w_out, tm=256, tn=512,
                             rows_per_mod=seq, resid=(xp, mp[2]), name="ssd_out")
            xs = _projection("gated", "resid", (y_s, z_s, ssd_norm_g[j]), w_out, tm=tms, tn=512,
                             rows_per_mod=1, resid=(xs, ms[2]), name="ssd_out")
            g_, n_ = SSD_GROUPS, SSD_STATE
            outs.setdefault("ssd_conv_p", []).append(xbc_p3[:, seq - (CONV_W - 1):])
            outs.setdefault("ssd_p", []).append(
                st_p.reshape(bp, g_, n_, heads // g_, SSD_HEAD_DIM).transpose(0, 1, 3, 4, 2))
            outs.setdefault("ssd_conv_s", []).append(
                jnp.concatenate([state_ssd_conv[j][:, 1:], xbc_s[:, None]], axis=1))
            outs.setdefault("ssd_s", []).append(st_s.reshape(state_ssd[j].shape))

        xp, xs = _peer_layer(
            [dict(x=xp, sc=mp[4], sh=mp[3], gate=mp[5], rows_per_mod=seq, tm=TM_PROMPT, tme=TM_EXPERT),
             dict(x=xs, sc=ms[4], sh=ms[3], gate=ms[5], rows_per_mod=1, tm=tms, tme=tms)],
            peer_w_q[l], peer_k1[l], peer_k2[l], u_bf, vt_bf, l, norm2_g[l])

    y_p = _final_norm(xp, final_g, TM_PROMPT).reshape(bp, seq, d)
    y_s = _final_norm(xs, final_g, tms).reshape(bs, 1, d)
    st = lambda name: outs[name][0][None] if len(outs[name]) == 1 else jnp.stack(outs[name])
    return (y_p, y_s, st("rg_conv_p"), st("rg_h_p"), st("ssd_conv_p"), st("ssd_p"),
            st("rg_conv_s"), st("rg_h_s"), st("ssd_conv_s"), st("ssd_s"))
```

```python
import functools
import math

import jax
import jax.numpy as jnp
from jax import lax
from jax.experimental import pallas as pl
from jax.experimental.pallas import tpu as pltpu

F32 = jnp.float32
BF16 = jnp.bfloat16

EPS = 1e-6
CONV_W = 4
RG_BLOCKS = 8
RG_C = 8.0
SSD_HEAD_DIM = 64
SSD_GROUPS = 8
SSD_STATE = 128
SSD_CHUNK = 128
PEER_HEADS = 8
PEER_NKEYS = 128
PEER_TOPK = 16
LANES = 128
SUBLANES = 8
VMEM_LIMIT = 56 * 1024 * 1024
NEG_INF = float("-inf")


def _params(*sem):
    return pltpu.CompilerParams(dimension_semantics=sem, vmem_limit_bytes=VMEM_LIMIT)


def _silu(x):
    return x * jax.nn.sigmoid(x)


def _softplus(x):
    return jnp.maximum(x, 0.0) + jnp.log1p(jnp.exp(-jnp.abs(x)))


def _rms(x):
    return x * lax.rsqrt(jnp.mean(x * x, axis=-1, keepdims=True) + EPS)


def _dot(a, b):
    return jnp.dot(a, b, preferred_element_type=F32)


def _dot_nt(a, b):
    return lax.dot_general(a, b, (((1,), (1,)), ((), ())), preferred_element_type=F32)


def _dot_tn(a, b):
    return lax.dot_general(a, b, (((0,), (0,)), ((), ())), preferred_element_type=F32)


def _mod_kernel(c_ref, w_ref, b_ref, o_ref):
    cs = _silu(c_ref[...]).astype(BF16)
    o_ref[...] = _dot(cs, w_ref[...].astype(BF16)) + b_ref[...]


def _modulation(c_all, w_mod, b_mod, chunk):
    depth, d, n = w_mod.shape
    rows = c_all.shape[0]
    tn = 512
    per = chunk // tn
    return pl.pallas_call(
        _mod_kernel,
        grid=(depth, n // tn),
        in_specs=[
            pl.BlockSpec((rows, d), lambda l, j: (0, 0)),
            pl.BlockSpec((None, d, tn), lambda l, j: (l, 0, j)),
            pl.BlockSpec((None, 1, tn), lambda l, j: (l, 0, j)),
        ],
        out_specs=pl.BlockSpec((None, None, rows, tn), lambda l, j: (l, j // per, 0, j % per)),
        out_shape=jax.ShapeDtypeStruct((depth, n // chunk, rows, chunk), F32),
        compiler_params=_params("arbitrary", "arbitrary"),
        name="modulation",
    )(c_all, w_mod, b_mod.reshape(depth, 1, n))


def _proj_kernel(*refs, pro, epi, emit_lhs, n_extra, out_dtype):
    it = iter(refs)
    if pro == "normmod":
        x_ref, g_ref, sc_ref, sh_ref = next(it), next(it), next(it), next(it)
    elif pro == "gated":
        y_ref, z_ref, g_ref = next(it), next(it), next(it)
    else:
        x_ref = next(it)
    w_ref = next(it)
    b_ref = next(it) if epi in ("bias", "resid") else None
    if epi == "resid":
        xres_ref, gate_ref = next(it), next(it)
    wx_ref = next(it) if n_extra else None
    o_ref = next(it)
    lhs_out_ref = next(it) if emit_lhs else None
    ox_ref = next(it) if n_extra else None
    lhs_ref = next(it) if pro != "plain" else None

    j = pl.program_id(1)

    if pro != "plain":
        @pl.when(j == 0)
        def _():
            if pro == "normmod":
                x = x_ref[...]
                h = _rms(x) * g_ref[...] * (1.0 + sc_ref[...]) + sh_ref[...]
            else:
                y = y_ref[...]
                h = _rms(y * _silu(z_ref[...])) * g_ref[...]
            hb = h.astype(BF16)
            lhs_ref[...] = hb
            if emit_lhs:
                lhs_out_ref[...] = hb
            if n_extra:
                ox_ref[...] = _dot(hb, wx_ref[...])
        lhs = lhs_ref[...]
    else:
        lhs = x_ref[...]

    acc = _dot(lhs, w_ref[...])
    if b_ref is not None:
        acc = acc + b_ref[...]
    if epi == "resid":
        acc = xres_ref[...] + gate_ref[...] * acc
    o_ref[...] = acc.astype(out_dtype)


def _projection(pro, epi, ins, w, *, tm, tn, rows_per_mod=None, bias=None, resid=None,
                w_extra=None, emit_lhs=False, out_dtype=F32, name="proj"):
    k, n = w.shape
    m = ins[0].shape[0]
    assert m % tm == 0 and n % tn == 0
    grid = (m // tm, n // tn)

    def mod_spec(width_block, col_of):
        if rows_per_mod == 1:
            return pl.BlockSpec((tm, width_block), lambda i, j: (i, col_of(j)))
        assert rows_per_mod % tm == 0
        t = rows_per_mod // tm
        return pl.BlockSpec((None, 1, width_block), lambda i, j: (i // t, 0, col_of(j)))

    def mod_arr(a):
        return a if rows_per_mod == 1 else a.reshape(a.shape[0], 1, a.shape[1])

    args, specs = [], []
    if pro == "normmod":
        x, g, sc, sh = ins
        args += [x, g.reshape(1, k), mod_arr(sc), mod_arr(sh)]
        specs += [pl.BlockSpec((tm, k), lambda i, j: (i, 0)),
                  pl.BlockSpec((1, k), lambda i, j: (0, 0)),
                  mod_spec(k, lambda j: 0), mod_spec(k, lambda j: 0)]
    elif pro == "gated":
        y, z, g = ins
        args += [y, z, g.reshape(1, k)]
        specs += [pl.BlockSpec((tm, k), lambda i, j: (i, 0)),
                  pl.BlockSpec((tm, k), lambda i, j: (i, 0)),
                  pl.BlockSpec((1, k), lambda i, j: (0, 0))]
    else:
        (x,) = ins
        args += [x]
        specs += [pl.BlockSpec((tm, k), lambda i, j: (i, 0))]
    args.append(w)
    specs.append(pl.BlockSpec((k, tn), lambda i, j: (0, j)))
    if epi in ("bias", "resid"):
        b = bias if bias is not None else jnp.zeros((n,), F32)
        args.append(b.reshape(1, n))
        specs.append(pl.BlockSpec((1, tn), lambda i, j: (0, j)))
    if epi == "resid":
        xres, gate = resid
        args += [xres, mod_arr(gate)]
        specs += [pl.BlockSpec((tm, tn), lambda i, j: (i, j)), mod_spec(tn, lambda j: j)]
    n_extra = 0
    if w_extra is not None:
        n_extra = w_extra.shape[1]
        args.append(w_extra)
        specs.append(pl.BlockSpec((k, n_extra), lambda i, j: (0, 0)))

    out_shapes = [jax.ShapeDtypeStruct((m, n), out_dtype)]
    out_specs = [pl.BlockSpec((tm, tn), lambda i, j: (i, j))]
    if emit_lhs:
        out_shapes.append(jax.ShapeDtypeStruct((m, k), BF16))
        out_specs.append(pl.BlockSpec((tm, k), lambda i, j: (i, 0)))
    if n_extra:
        out_shapes.append(jax.ShapeDtypeStruct((m, n_extra), F32))
        out_specs.append(pl.BlockSpec((tm, n_extra), lambda i, j: (i, 0)))
    scratch = [pltpu.VMEM((tm, k), BF16)] if pro != "plain" else []

    outs = pl.pallas_call(
        functools.partial(_proj_kernel, pro=pro, epi=epi, emit_lhs=emit_lhs, n_extra=n_extra,
                          out_dtype=out_dtype),
        grid=grid, in_specs=specs, out_specs=out_specs, out_shape=out_shapes,
        scratch_shapes=scratch,
        compiler_params=_params("arbitrary", "arbitrary"),
        name=name,
    )(*args)
    return outs if len(outs) > 1 else outs[0]


def _rglru_gates(xc, wa_ref, ba_ref, wi_ref, bi_ref, lam_ref):
    xcb = xc.astype(BF16)
    r = jax.nn.sigmoid(_dot(xcb, wa_ref[...]) + ba_ref[...])
    i = jax.nn.sigmoid(_dot(xcb, wi_ref[...]) + bi_ref[...])
    log_a = (-RG_C * _softplus(-lam_ref[...])) * r
    a = jnp.exp(log_a)
    mult = jnp.sqrt(jnp.maximum(-jnp.tanh(log_a) * (a * a + 1.0), 0.0))
    return a, mult * (i * xc)


RG_ROWS = 256


def _rglru_seq_kernel(gate_ref, xr_ref, cw_ref, cb_ref, wa_ref, ba_ref, wi_ref, bi_ref, lam_ref,
                      y_ref, hlast_ref, xpad_ref):
    seq, blk = xr_ref.shape
    pad = SUBLANES
    xpad_ref[0:pad, :] = jnp.zeros((pad, blk), F32)
    xpad_ref[pad:pad + seq, :] = xr_ref[...]
    row_in_tile = lax.broadcasted_iota(jnp.int32, (RG_ROWS, blk), 0) % SUBLANES

    def step(c, carry):
        start = pl.multiple_of(c * RG_ROWS, RG_ROWS)
        xe = xpad_ref[pl.ds(start, RG_ROWS + pad), :]
        xc = cb_ref[...] + cw_ref[3:4, :] * xe[pad:]
        for k in range(1, CONV_W):
            xc = xc + cw_ref[3 - k:4 - k, :] * pltpu.roll(xe, k, axis=0)[pad:]
        a, b = _rglru_gates(xc, wa_ref, ba_ref, wi_ref, bi_ref, lam_ref)
        for s in (1, 2, 4):
            m = row_in_tile >= s
            b = jnp.where(m, a * pltpu.roll(b, s, axis=0) + b, b)
            a = jnp.where(m, a * pltpu.roll(a, s, axis=0), a)
        hs = []
        for t in range(RG_ROWS // SUBLANES):
            sl = slice(t * SUBLANES, (t + 1) * SUBLANES)
            h = b[sl] + a[sl] * carry
            hs.append(h)
            carry = h[SUBLANES - 1:SUBLANES, :]
        h_all = jnp.concatenate(hs, axis=0)
        y_ref[pl.ds(start, RG_ROWS), :] = (h_all * jax.nn.gelu(gate_ref[pl.ds(start, RG_ROWS), :])).astype(BF16)
        return carry

    carry = lax.fori_loop(0, seq // RG_ROWS, step, jnp.zeros((1, blk), F32))
    hlast_ref[...] = carry


def _rglru_seq(proj, conv_w, conv_b, w_a, b_a, w_i, b_i, lam):
    nb, seq, two_d = proj.shape
    d = two_d // 2
    blk = d // RG_BLOCKS
    vec = lambda a: a.reshape(1, d)
    vspec = pl.BlockSpec((1, blk), lambda b, k: (0, k))
    wspec = pl.BlockSpec((None, blk, blk), lambda b, k: (k, 0, 0))
    return pl.pallas_call(
        _rglru_seq_kernel,
        grid=(nb, RG_BLOCKS),
        in_specs=[
            pl.BlockSpec((None, seq, blk), lambda b, k: (b, 0, k)),
            pl.BlockSpec((None, seq, blk), lambda b, k: (b, 0, RG_BLOCKS + k)),
            pl.BlockSpec((CONV_W, blk), lambda b, k: (0, k)),
            vspec, wspec, vspec, wspec, vspec, vspec,
        ],
        out_specs=[
            pl.BlockSpec((None, seq, blk), lambda b, k: (b, 0, k)),
            pl.BlockSpec((None, 1, blk), lambda b, k: (b, 0, k)),
        ],
        out_shape=[jax.ShapeDtypeStruct((nb, seq, d), BF16), jax.ShapeDtypeStruct((nb, 1, d), F32)],
        scratch_shapes=[pltpu.VMEM((seq + SUBLANES, blk), F32)],
        compiler_params=_params("arbitrary", "arbitrary"),
        name="rglru_seq",
    )(proj, proj, conv_w, vec(conv_b), w_a.astype(BF16), vec(b_a), w_i.astype(BF16), vec(b_i), vec(lam))


def _rglru_step_kernel(gate_ref, xr_ref, c0_ref, c1_ref, c2_ref, h0_ref, cw_ref, cb_ref,
                       wa_ref, ba_ref, wi_ref, bi_ref, lam_ref, y_ref, h_ref):
    xc = (cb_ref[...] + cw_ref[0:1, :] * c0_ref[...] + cw_ref[1:2, :] * c1_ref[...]
          + cw_ref[2:3, :] * c2_ref[...] + cw_ref[3:4, :] * xr_ref[...])
    a, b = _rglru_gates(xc, wa_ref, ba_ref, wi_ref, bi_ref, lam_ref)
    h = a * h0_ref[...] + b
    h_ref[...] = h
    y_ref[...] = (h * jax.nn.gelu(gate_ref[...])).astype(BF16)


def _rglru_step(proj, conv_state, h0, conv_w, conv_b, w_a, b_a, w_i, b_i, lam):
    nb, two_d = proj.shape
    d = two_d // 2
    blk = d // RG_BLOCKS
    vec = lambda a: a.reshape(1, d)
    rows = pl.BlockSpec((nb, blk), lambda k: (0, k))
    vspec = pl.BlockSpec((1, blk), lambda k: (0, k))
    wspec = pl.BlockSpec((None, blk, blk), lambda k: (k, 0, 0))
    return pl.pallas_call(
        _rglru_step_kernel,
        grid=(RG_BLOCKS,),
        in_specs=[rows, pl.BlockSpec((nb, blk), lambda k: (0, RG_BLOCKS + k)), rows, rows, rows, rows,
                  pl.BlockSpec((CONV_W, blk), lambda k: (0, k)),
                  vspec, wspec, vspec, wspec, vspec, vspec],
        out_specs=[rows, rows],
        out_shape=[jax.ShapeDtypeStruct((nb, d), BF16), jax.ShapeDtypeStruct((nb, d), F32)],
        compiler_params=_params("arbitrary"),
        name="rglru_step",
    )(proj, proj, conv_state[:, 0], conv_state[:, 1], conv_state[:, 2], h0, conv_w, vec(conv_b),
      w_a.astype(BF16), vec(b_a), w_i.astype(BF16), vec(b_i), vec(lam))


def _split3(x):
    hi = x.astype(BF16)
    r1 = x - hi.astype(F32)
    mid = r1.astype(BF16)
    lo = (r1 - mid.astype(F32)).astype(BF16)
    return hi, mid, lo


def _head_expand(x, expand_ref):
    hi = x.astype(BF16)
    lo = (x - hi.astype(F32)).astype(BF16)
    return _dot(hi, expand_ref[...]) + _dot(lo, expand_ref[...])


def _ssd_seq_kernel(xbc_ref, halo_ref, dt_ref, z_ref, cw_ref, cb_ref, dtb_ref, alog_ref, drep_ref, g_ref,
                    expand_ref, yn_ref, st_ref, *, d_inner):
    c = pl.program_id(1)
    q = SSD_CHUNK
    p = SSD_HEAD_DIM
    n = SSD_STATE
    hpg = d_inner // p // SSD_GROUPS
    epp = hpg * p
    pad = SUBLANES

    @pl.when(c == 0)
    def _():
        st_ref[...] = jnp.zeros(st_ref.shape, F32)

    halo = jnp.where(c > 0, halo_ref[...], 0.0)
    xe = jnp.concatenate([halo, xbc_ref[...]], axis=0)
    xc = cb_ref[...] + cw_ref[3:4, :] * xe[pad:]
    for k in range(1, CONV_W):
        xc = xc + cw_ref[3 - k:4 - k, :] * pltpu.roll(xe, k, axis=0)[pad:]
    xc = _silu(xc)
    xs = xc[:, :d_inner]

    dt = _softplus(dt_ref[...] + dtb_ref[...])
    da = dt * (-jnp.exp(alog_ref[...]))
    ri = lax.broadcasted_iota(jnp.int32, (q, q), 0)
    ci = lax.broadcasted_iota(jnp.int32, (q, q), 1)
    tri = ri >= ci
    ltri = jnp.where(tri, 1.0, 0.0).astype(BF16)
    hi, mid, lo = _split3(da)
    cum = _dot(ltri, hi) + _dot(ltri, mid) + _dot(ltri, lo)
    cum_t = cum.T
    dt_t = dt.T
    last = cum[q - 1:q, :]
    ecum_x = _head_expand(jnp.exp(cum), expand_ref)
    wend_x = _head_expand(jnp.exp(last - cum) * dt, expand_ref)
    elast_x = _head_expand(jnp.broadcast_to(jnp.exp(last), (SUBLANES, last.shape[1])), expand_ref)[0:1, :]
    xw = (xs * wend_x).astype(BF16)
    first_head = lax.broadcasted_iota(jnp.int32, (q, 2 * p), 1) < p

    y_groups = []
    for g in range(SSD_GROUPS):
        bb = xc[:, d_inner + g * n:d_inner + (g + 1) * n].astype(BF16)
        cc = xc[:, d_inner + SSD_GROUPS * n + g * n:d_inner + SSD_GROUPS * n + (g + 1) * n].astype(BF16)
        cb = _dot_nt(cc, bb)
        st = st_ref[g]
        ys = _dot(cc, st.astype(BF16))
        pairs = []
        for e in range(0, hpg, 2):
            xpair = xs[:, (g * hpg + e) * p:(g * hpg + e + 2) * p].astype(BF16)
            acc = None
            for k in range(2):
                h = g * hpg + e + k
                decay = jnp.exp(jnp.where(tri, cum[:, h:h + 1] - cum_t[h:h + 1, :], NEG_INF))
                w = (decay * cb * dt_t[h:h + 1, :]).astype(BF16)
                keep = first_head if k == 0 else jnp.logical_not(first_head)
                term = _dot(w, jnp.where(keep, xpair, jnp.zeros((), BF16)))
                acc = term if acc is None else acc + term
            pairs.append(acc)
        lanes = slice(g * epp, (g + 1) * epp)
        y_groups.append(jnp.concatenate(pairs, axis=1) + ys * ecum_x[:, lanes])
        st_ref[g] = st * elast_x[:, lanes] + _dot_tn(bb, xw[:, lanes])
    y = jnp.concatenate(y_groups, axis=1) + drep_ref[...] * xs
    yn_ref[...] = (_rms(y * _silu(z_ref[...])) * g_ref[...]).astype(BF16)


def _ssd_seq(xbc, dt_raw, z, conv_w, conv_b, dt_bias, a_log, d_skip, norm_g, d_inner):
    nb, seq, cdim = xbc.shape
    q = SSD_CHUNK
    hp = dt_raw.shape[-1]
    heads = d_inner // SSD_HEAD_DIM
    epp = d_inner // SSD_GROUPS
    padv = lambda a: jnp.pad(a.reshape(1, heads), ((0, 0), (0, hp - heads)))
    halo_blocks = q // SUBLANES
    expand = (jnp.arange(d_inner)[None, :] // SSD_HEAD_DIM == jnp.arange(hp)[:, None]).astype(BF16)
    const = lambda shape: pl.BlockSpec(shape, lambda b, c: (0, 0))
    return pl.pallas_call(
        functools.partial(_ssd_seq_kernel, d_inner=d_inner),
        grid=(nb, seq // q),
        in_specs=[
            pl.BlockSpec((None, q, cdim), lambda b, c: (b, c, 0)),
            pl.BlockSpec((None, SUBLANES, cdim), lambda b, c: (b, jnp.maximum(c * halo_blocks - 1, 0), 0)),
            pl.BlockSpec((None, q, hp), lambda b, c: (b, c, 0)),
            pl.BlockSpec((None, q, d_inner), lambda b, c: (b, c, 0)),
            const((CONV_W, cdim)), const((1, cdim)), const((1, hp)), const((1, hp)),
            const((1, d_inner)), const((1, d_inner)), const((hp, d_inner)),
        ],
        out_specs=[
            pl.BlockSpec((None, q, d_inner), lambda b, c: (b, c, 0)),
            pl.BlockSpec((None, SSD_GROUPS, SSD_STATE, epp), lambda b, c: (b, 0, 0, 0)),
        ],
        out_shape=[jax.ShapeDtypeStruct((nb, seq, d_inner), BF16),
                   jax.ShapeDtypeStruct((nb, SSD_GROUPS, SSD_STATE, epp), F32)],
        compiler_params=_params("arbitrary", "arbitrary"),
        name="ssd_seq",
    )(xbc, xbc, dt_raw, z, conv_w, conv_b.reshape(1, cdim), padv(dt_bias), padv(a_log),
      jnp.repeat(d_skip, SSD_HEAD_DIM).reshape(1, d_inner), norm_g.reshape(1, d_inner), expand)


def _ssd_step_pre_kernel(xbc_ref, c0_ref, c1_ref, c2_ref, dt_ref, cw_ref, cb_ref, dtb_ref, alog_ref, drep_ref,
                         xdt_ref, skip_ref, bc_ref, dec_ref, *, d_inner):
    xc = (cb_ref[...] + cw_ref[0:1, :] * c0_ref[...] + cw_ref[1:2, :] * c1_ref[...]
          + cw_ref[2:3, :] * c2_ref[...] + cw_ref[3:4, :] * xbc_ref[...])
    xc = _silu(xc)
    dt = _softplus(dt_ref[...] + dtb_ref[...])
    dec_ref[...] = jnp.exp(dt * (-jnp.exp(alog_ref[...])))
    hp = dt.shape[1]
    head_of_lane = lax.broadcasted_iota(jnp.int32, (hp, d_inner), 1) // SSD_HEAD_DIM
    expand = jnp.where(head_of_lane == lax.broadcasted_iota(jnp.int32, (hp, d_inner), 0), 1.0, 0.0).astype(BF16)
    dt_rep = sum(_dot(piece, expand) for piece in _split3(dt))
    xs = xc[:, :d_inner]
    xdt_ref[...] = xs * dt_rep
    skip_ref[...] = xs * drep_ref[...]
    bc_ref[...] = xc[:, d_inner:]


def _ssd_step_state_kernel(dec_ref, s_ref, xdt_ref, skip_ref, b_ref, c_ref, s_out_ref, y_ref):
    tb, npair, rows, n = s_ref.shape
    pairs_per_group = npair // SSD_GROUPS
    seq0 = pl.program_id(0) * tb
    row_id = lax.broadcasted_iota(jnp.int32, (rows, n), 0)
    pair_id = lax.broadcasted_iota(jnp.int32, (npair, n), 0)
    for b in range(tb):
        xt = xdt_ref[b].T.astype(BF16)
        bm = b_ref[b]
        brep = jnp.concatenate([jnp.broadcast_to(bm[g:g + 1], (pairs_per_group, n))
                                for g in range(SSD_GROUPS)], axis=0)
        bbig = jnp.concatenate([jnp.where(pair_id == j, brep, 0.0) for j in range(npair)], axis=1)
        upd = _dot(xt, bbig.astype(BF16))
        cm = c_ref[b].astype(BF16)
        for j in range(npair):
            g = j // pairs_per_group
            d0 = dec_ref[seq0 + b, 2 * j]
            d1 = dec_ref[seq0 + b, 2 * j + 1]
            s_new = s_ref[b, j] * jnp.where(row_id < rows // 2, d0, d1) + upd[:, j * n:(j + 1) * n]
            s_out_ref[b, j] = s_new
            y8 = _dot_nt(cm, s_new.astype(BF16))
            y_ref[b, j:j + 1, :] = y8[g:g + 1, :] + skip_ref[b, j:j + 1, :]


def _ssd_step(xbc_raw, conv_state, dt_raw, state, conv_w, conv_b, dt_bias, a_log, d_skip, d_inner):
    nb, cdim = xbc_raw.shape
    hp = dt_raw.shape[-1]
    heads = d_inner // SSD_HEAD_DIM
    p, n = SSD_HEAD_DIM, SSD_STATE
    gn = SSD_GROUPS * n
    padv = lambda a: jnp.pad(a.reshape(1, heads), ((0, 0), (0, hp - heads)))
    full = lambda shape: pl.BlockSpec(shape, lambda: tuple(0 for _ in shape))
    xdt, skip, bc, dec = pl.pallas_call(
        functools.partial(_ssd_step_pre_kernel, d_inner=d_inner),
        in_specs=[full((nb, cdim))] * 4 + [full((nb, hp)), full((CONV_W, cdim)), full((1, cdim)),
                                            full((1, hp)), full((1, hp)), full((1, d_inner))],
        out_specs=[full((nb, d_inner)), full((nb, d_inner)), full((nb, 2 * gn)), full((nb, hp))],
        out_shape=[jax.ShapeDtypeStruct((nb, d_inner), F32), jax.ShapeDtypeStruct((nb, d_inner), F32),
                   jax.ShapeDtypeStruct((nb, 2 * gn), F32), jax.ShapeDtypeStruct((nb, hp), F32)],
        compiler_params=pltpu.CompilerParams(vmem_limit_bytes=VMEM_LIMIT),
        name="ssd_step_pre",
    )(xbc_raw, conv_state[:, 0], conv_state[:, 1], conv_state[:, 2], dt_raw, conv_w,
      conv_b.reshape(1, cdim), padv(dt_bias), padv(a_log), jnp.repeat(d_skip, p).reshape(1, d_inner))
    npair = heads // 2
    pair3 = lambda a: a.reshape(nb, npair, 2 * p)
    tb = 2
    sspec = pl.BlockSpec((tb, npair, 2 * p, n), lambda i: (i, 0, 0, 0))
    vspec = pl.BlockSpec((tb, npair, 2 * p), lambda i: (i, 0, 0))
    gspec = pl.BlockSpec((tb, SSD_GROUPS, n), lambda i: (i, 0, 0))
    s_new, y3 = pl.pallas_call(
        _ssd_step_state_kernel,
        grid=(nb // tb,),
        in_specs=[pl.BlockSpec(memory_space=pltpu.SMEM), sspec, vspec, vspec, gspec, gspec],
        out_specs=[sspec, vspec],
        out_shape=[jax.ShapeDtypeStruct((nb, npair, 2 * p, n), F32),
                   jax.ShapeDtypeStruct((nb, npair, 2 * p), F32)],
        compiler_params=_params("arbitrary"),
        name="ssd_step_state",
    )(dec, state.reshape(nb, npair, 2 * p, n), pair3(xdt), pair3(skip),
      bc[:, :gn].reshape(nb, SSD_GROUPS, n), bc[:, gn:].reshape(nb, SSD_GROUPS, n))
    return y3.reshape(nb, d_inner), s_new


NOT_RANKED = 99.0


def _tree(op, xs):
    xs = list(xs)
    while len(xs) > 1:
        xs = [op(xs[i], xs[i + 1]) if i + 1 < len(xs) else xs[i] for i in range(0, len(xs), 2)]
    return xs[0]
PEER_PAIRS = [(a, b) for a in range(PEER_TOPK) for b in range(PEER_TOPK) if (a + 1) * (b + 1) <= PEER_TOPK]


def _peer_prep_kernel(q_ref, k1_ref, k2_ref, r2_ref, e2_ref, c1_ref, e1_ref,
                      s_scr, work_scr, rank_scr, v_scr):
    tt = q_ref.shape[0]
    hh = PEER_HEADS
    half = k1_ref.shape[1]
    tile = lambda k: slice(k * hh, (k + 1) * hh)

    for side, k_ref in enumerate((k1_ref, k2_ref)):
        s = _dot_nt(k_ref[...], q_ref[:, side * half:(side + 1) * half])
        s_scr[side] = s
        work_scr[side] = s
        rank_scr[side] = jnp.full(s.shape, NOT_RANKED, F32)

    def extract(a, _):
        af = jnp.asarray(a, F32)
        for side in range(2):
            w = [work_scr[side, tile(k), :] for k in range(PEER_NKEYS)]
            m = _tree(jnp.maximum, w)
            sel = _tree(jnp.minimum, [jnp.where(w[k] == m, k, PEER_NKEYS) for k in range(PEER_NKEYS)])
            for k in range(PEER_NKEYS):
                hit = sel == k
                work_scr[side, tile(k), :] = jnp.where(hit, NEG_INF, w[k])
                rank_scr[side, tile(k), :] = jnp.where(hit, af, rank_scr[side, tile(k), :])
            v_scr[side, a] = m
        return 0

    lax.fori_loop(0, PEER_TOPK, extract, 0)

    v1 = [v_scr[0, a] for a in range(PEER_TOPK)]
    v2 = [v_scr[1, b] for b in range(PEER_TOPK)]
    cand = {c: v1[c[0]] + v2[c[1]] for c in PEER_PAIRS}
    ahead_of = {c: [] for c in PEER_PAIRS}
    behind_of = {c: [] for c in PEER_PAIRS}
    for n1, c in enumerate(PEER_PAIRS):
        for c2 in PEER_PAIRS[n1 + 1:]:
            if c2[0] >= c[0] and c2[1] >= c[1]:
                continue
            ahead = jnp.where(cand[c] >= cand[c2], 1.0, 0.0)
            ahead_of[c2].append(ahead)
            behind_of[c].append(ahead)
    e1v = [jnp.exp(v1[a] - v1[0]) for a in range(PEER_TOPK)]
    e2v = [jnp.exp(v2[b] - v2[0]) for b in range(PEER_TOPK)]
    keep_terms = [[] for _ in range(PEER_TOPK)]
    z_terms = []
    for c in PEER_PAIRS:
        cnt = jnp.full((hh, tt), float((c[0] + 1) * (c[1] + 1) - 1 + len(behind_of[c])), F32)
        if ahead_of[c]:
            cnt = cnt + _tree(jnp.add, ahead_of[c])
        if behind_of[c]:
            cnt = cnt - _tree(jnp.add, behind_of[c])
        keep = jnp.where(cnt < float(PEER_TOPK), 1.0, 0.0)
        keep_terms[c[0]].append(keep)
        z_terms.append(keep * (e1v[c[0]] * e2v[c[1]]))
    ncols = [_tree(jnp.add, t) for t in keep_terms]
    z = _tree(jnp.add, z_terms)
    rz = 1.0 / z

    twice = lambda a: jnp.concatenate([a, a], axis=0)
    ncols2 = [twice(c) for c in ncols]
    m1, m2, rz2 = twice(v1[0]), twice(v2[0]), twice(rz)
    for k in range(0, PEER_NKEYS, 2):
        rows = slice(k * hh, (k + 2) * hh)
        r1 = rank_scr[0, rows, :]
        c1 = jnp.zeros((2 * hh, tt), F32)
        for a in range(PEER_TOPK):
            c1 = jnp.where(r1 == float(a), ncols2[a], c1)
        c1_ref[rows, :] = c1
        e1_ref[rows, :] = jnp.exp(s_scr[0, rows, :] - m1) * rz2
        e2_ref[rows, :] = jnp.exp(s_scr[1, rows, :] - m2).astype(BF16)
        r2_ref[rows, :] = rank_scr[1, rows, :].astype(BF16)


def _peer_prep(qp, k1big, k2big, tt=LANES):
    t = qp.shape[0]
    rows = k1big.shape[0]
    out = lambda dt: jax.ShapeDtypeStruct((rows, t), dt)
    ospec = pl.BlockSpec((rows, tt), lambda i: (0, i))
    return pl.pallas_call(
        _peer_prep_kernel,
        grid=(t // tt,),
        in_specs=[pl.BlockSpec((tt, qp.shape[1]), lambda i: (i, 0)),
                  pl.BlockSpec(k1big.shape, lambda i: (0, 0)),
                  pl.BlockSpec(k2big.shape, lambda i: (0, 0))],
        out_specs=[ospec] * 4,
        out_shape=[out(BF16), out(BF16), out(F32), out(F32)],
        scratch_shapes=[pltpu.VMEM((2, rows, tt), F32), pltpu.VMEM((2, rows, tt), F32),
                        pltpu.VMEM((2, rows, tt), F32), pltpu.VMEM((2, PEER_TOPK, PEER_HEADS, tt), F32)],
        compiler_params=_params("arbitrary"),
        name="peer_prep",
    )(qp, k1big, k2big)


def _peer_expert_kernel(hc_ref, u_ref, vt_ref, r2_ref, e2_ref, c1_ref, e1_ref, x_ref, gate_ref,
                        o_ref, acc_scr):
    e = pl.program_id(1)
    te = u_ref.shape[0]
    nk = PEER_NKEYS

    @pl.when(e == 0)
    def _():
        acc_scr[...] = jnp.zeros(acc_scr.shape, F32)

    act = jax.nn.gelu(_dot_nt(u_ref[...], hc_ref[...])).astype(BF16)
    blocks = []
    for ii in range(te // nk):
        w = None
        for h in range(PEER_HEADS):
            row = ii * PEER_HEADS + h
            c1 = c1_ref[row:row + 1, :].astype(BF16)
            e1 = e1_ref[row:row + 1, :].astype(BF16)
            term = jnp.where(r2_ref[h] < c1, e2_ref[h], jnp.zeros((), BF16)) * e1
            w = term if w is None else w + term
        blocks.append(act[ii * nk:(ii + 1) * nk] * w)
    acc_scr[...] += _dot(vt_ref[...], jnp.concatenate(blocks, axis=0))

    @pl.when(e == pl.num_programs(1) - 1)
    def _():
        o_ref[...] = x_ref[...] + gate_ref[...] * acc_scr[...].T


def _peer_expert(hc, u_bf, vt_bf, layer, r2, e2, c1, e1, x, gate, *, tm, te, rows_per_mod):
    t, d = hc.shape
    n_exp = u_bf.shape[1]
    rows_i = te // PEER_NKEYS * PEER_HEADS
    hh, nk = PEER_HEADS, PEER_NKEYS
    once = pl.Buffered(1)
    if rows_per_mod == 1:
        gate_arr = gate
        gate_spec = pl.BlockSpec((tm, d), lambda i, e: (i, 0), pipeline_mode=once)
    else:
        per = rows_per_mod // tm
        gate_arr = gate.reshape(gate.shape[0], 1, d)
        gate_spec = pl.BlockSpec((None, 1, d), lambda i, e: (i // per, 0, 0))
    return pl.pallas_call(
        _peer_expert_kernel,
        grid=(t // tm, n_exp // te),
        in_specs=[
            pl.BlockSpec((tm, d), lambda i, e: (i, 0), pipeline_mode=once),
            pl.BlockSpec((None, te, d), lambda i, e: (layer, e, 0)),
            pl.BlockSpec((None, d, te), lambda i, e: (layer, 0, e)),
            pl.BlockSpec((hh, nk, tm), lambda i, e: (0, 0, i), pipeline_mode=once),
            pl.BlockSpec((hh, nk, tm), lambda i, e: (0, 0, i), pipeline_mode=once),
            pl.BlockSpec((rows_i, tm), lambda i, e: (e, i)),
            pl.BlockSpec((rows_i, tm), lambda i, e: (e, i)),
            pl.BlockSpec((tm, d), lambda i, e: (i, 0), pipeline_mode=once),
            gate_spec,
        ],
        out_specs=pl.BlockSpec((tm, d), lambda i, e: (i, 0)),
        out_shape=jax.ShapeDtypeStruct((t, d), F32),
        scratch_shapes=[pltpu.VMEM((d, tm), F32)],
        compiler_params=_params("arbitrary", "arbitrary"),
        name="peer_expert",
    )(hc, u_bf, vt_bf, r2, e2, c1, e1, x, gate_arr)


def _final_norm_kernel(x_ref, g_ref, o_ref):
    o_ref[...] = _rms(x_ref[...]) * g_ref[...]


def _final_norm(x, g, tm):
    t, d = x.shape
    return pl.pallas_call(
        _final_norm_kernel,
        grid=(t // tm,),
        in_specs=[pl.BlockSpec((tm, d), lambda i: (i, 0)), pl.BlockSpec((1, d), lambda i: (0, 0))],
        out_specs=pl.BlockSpec((tm, d), lambda i: (i, 0)),
        out_shape=jax.ShapeDtypeStruct((t, d), F32),
        compiler_params=_params("arbitrary"),
        name="final_norm",
    )(x, g.reshape(1, d))


TM_PROMPT = 1024
TN_PROJ = 1024
TM_EXPERT = 512
TE_EXPERT = 1024


def _peer_layer(groups, w_q, k1, k2, u_bf, vt_bf, layer, norm_g):
    d = w_q.shape[0]
    hh, nk = PEER_HEADS, PEER_NKEYS
    half = k1.shape[1]
    wq = w_q.reshape(d, hh, 2, half).transpose(0, 2, 1, 3).reshape(d, 2 * hh * half).astype(BF16)
    eye = jnp.eye(hh, dtype=F32)
    kbig = lambda k: jnp.einsum("kd,hg->khgd", k, eye).reshape(nk * hh, hh * half).astype(BF16)
    k1b, k2b = kbig(k1), kbig(k2)
    outs = []
    for gr in groups:
        x = gr["x"]
        t = x.shape[0]
        qp, hc = _projection("normmod", "none", (x, norm_g, gr["sc"], gr["sh"]), wq,
                             tm=gr["tm"], tn=TN_PROJ, rows_per_mod=gr["rows_per_mod"],
                             emit_lhs=True, out_dtype=BF16, name="peer_q")
        r2, e2, c1, e1 = _peer_prep(qp, k1b, k2b)
        r2, e2 = [a.reshape(nk, hh, t).transpose(1, 0, 2) for a in (r2, e2)]
        outs.append(_peer_expert(hc, u_bf, vt_bf, layer, r2, e2, c1, e1, x, gr["gate"],
                                 tm=gr["tme"], te=TE_EXPERT, rows_per_mod=gr["rows_per_mod"]))
    return outs


def kernel(x_prompt, x_sample, state_rg_conv, state_rg_h, state_ssd_conv, state_ssd, c_prompt, c_sample, norm1_g, norm2_g, w_mod, b_mod, rg_w_in, rg_b_in, rg_conv_w, rg_conv_b, rg_w_a, rg_b_a, rg_w_i, rg_b_i, rg_lambda, rg_w_out, rg_b_out, ssd_w_in, ssd_conv_w, ssd_conv_b, ssd_dt_bias, ssd_a_log, ssd_d, ssd_norm_g, ssd_w_out, peer_w_q, peer_k1, peer_k2, peer_u, peer_v, final_g):
    bp, seq, d = x_prompt.shape
    bs = x_sample.shape[0]
    tp = bp * seq
    depth = w_mod.shape[0]
    d_rnn = rg_w_out.shape[1]
    d_inner = ssd_w_out.shape[1]
    cdim = ssd_conv_w.shape[-1]
    heads = d_inner // SSD_HEAD_DIM

    nc = bp + bs
    c_all = jnp.concatenate([c_sample, c_prompt, jnp.zeros((-nc % SUBLANES, d), F32)], axis=0)
    mod = _modulation(c_all, w_mod, b_mod, d)
    u_bf = peer_u.astype(BF16)
    vt_bf = jnp.swapaxes(peer_v, 1, 2).astype(BF16)

    xp = x_prompt.reshape(tp, d)
    xs = x_sample.reshape(bs, d)
    tms = bs
    outs = {}

    for l in range(depth):
        mp = [mod[l, i, bs:nc] for i in range(6)]
        ms = [mod[l, i, :bs] for i in range(6)]
        j = l // 2
        if l % 2 == 0:
            w_in = rg_w_in[j].astype(BF16)
            w_out = rg_w_out[j].astype(BF16)
            rg = (rg_conv_w[j], rg_conv_b[j], rg_w_a[j], rg_b_a[j], rg_w_i[j], rg_b_i[j], rg_lambda[j])
            proj_p = _projection("normmod", "bias", (xp, norm1_g[l], mp[1], mp[0]), w_in, tm=TM_PROMPT,
                                 tn=TN_PROJ, rows_per_mod=seq, bias=rg_b_in[j], name="rg_in")
            proj_s = _projection("normmod", "bias", (xs, norm1_g[l], ms[1], ms[0]), w_in, tm=tms,
                                 tn=TN_PROJ, rows_per_mod=1, bias=rg_b_in[j], name="rg_in")
            proj_p3 = proj_p.reshape(bp, seq, 2 * d_rnn)
            y_p, h_p = _rglru_seq(proj_p3, *rg)
            y_s, h_s = _rglru_step(proj_s, state_rg_conv[j], state_rg_h[j], *rg)
            xp = _projection("plain", "resid", (y_p.reshape(tp, d_rnn),), w_out, tm=TM_PROMPT, tn=TN_PROJ,
                             rows_per_mod=seq, bias=rg_b_out[j], resid=(xp, mp[2]), name="rg_out")
            xs = _projection("plain", "resid", (y_s,), w_out, tm=tms, tn=TN_PROJ,
                             rows_per_mod=1, bias=rg_b_out[j], resid=(xs, ms[2]), name="rg_out")
            outs.setdefault("rg_conv_p", []).append(proj_p3[:, seq - (CONV_W - 1):, d_rnn:])
            outs.setdefault("rg_h_p", []).append(h_p.reshape(bp, d_rnn))
            outs.setdefault("rg_conv_s", []).append(
                jnp.concatenate([state_rg_conv[j][:, 1:], proj_s[:, None, d_rnn:]], axis=1))
            outs.setdefault("rg_h_s", []).append(h_s)
        else:
            w_in = ssd_w_in[j]
            w_z = w_in[:, :d_inner].astype(BF16)
            w_xbc = w_in[:, d_inner:d_inner + cdim].astype(BF16)
            w_dt = jnp.pad(w_in[:, d_inner + cdim:], ((0, 0), (0, LANES - heads))).astype(BF16)
            w_out = ssd_w_out[j].astype(BF16)
            sp = (ssd_conv_w[j], ssd_conv_b[j], ssd_dt_bias[j], ssd_a_log[j], ssd_d[j])
            z_p, hm_p, dt_p = _projection("normmod", "none", (xp, norm1_g[l], mp[1], mp[0]), w_z, tm=TM_PROMPT,
                                          tn=TN_PROJ, rows_per_mod=seq, w_extra=w_dt, emit_lhs=True, name="ssd_in_z")
            z_s, hm_s, dt_s = _projection("normmod", "none", (xs, norm1_g[l], ms[1], ms[0]), w_z, tm=tms,
                                          tn=TN_PROJ, rows_per_mod=1, w_extra=w_dt, emit_lhs=True, name="ssd_in_z")
            xbc_p = _projection("plain", "none", (hm_p,), w_xbc, tm=TM_PROMPT, tn=TN_PROJ, name="ssd_in_xbc")
            xbc_s = _projection("plain", "none", (hm_s,), w_xbc, tm=tms, tn=TN_PROJ, name="ssd_in_xbc")
            xbc_p3 = xbc_p.reshape(bp, seq, cdim)
            yn_p, st_p = _ssd_seq(xbc_p3, dt_p.reshape(bp, seq, LANES), z_p.reshape(bp, seq, d_inner), *sp,
                                  ssd_norm_g[j], d_inner)
            y_s, st_s = _ssd_step(xbc_s, state_ssd_conv[j], dt_s, state_ssd[j], *sp, d_inner)
            xp = _projection("plain", "resid", (yn_p.reshape(tp, d_inner),), w_out, tm=TM_PROMPT, tn=TN_PROJ,
                             rows_per_mod=seq, resid=(xp, mp[2]), name="ssd_out")
            xs = _projection("gated", "resid", (y_s, z_s, ssd_norm_g[j]), w_out, tm=tms, tn=512,
                             rows_per_mod=1, resid=(xs, ms[2]), name="ssd_out")
            g_, n_ = SSD_GROUPS, SSD_STATE
            outs.setdefault("ssd_conv_p", []).append(xbc_p3[:, seq - (CONV_W - 1):])
            outs.setdefault("ssd_p", []).append(
                st_p.reshape(bp, g_, n_, heads // g_, SSD_HEAD_DIM).transpose(0, 1, 3, 4, 2))
            outs.setdefault("ssd_conv_s", []).append(
                jnp.concatenate([state_ssd_conv[j][:, 1:], xbc_s[:, None]], axis=1))
            outs.setdefault("ssd_s", []).append(st_s.reshape(state_ssd[j].shape))

        xp, xs = _peer_layer(
            [dict(x=xp, sc=mp[4], sh=mp[3], gate=mp[5], rows_per_mod=seq, tm=TM_PROMPT, tme=TM_EXPERT),
             dict(x=xs, sc=ms[4], sh=ms[3], gate=ms[5], rows_per_mod=1, tm=tms, tme=tms)],
            peer_w_q[l], peer_k1[l], peer_k2[l], u_bf, vt_bf, l, norm2_g[l])

    y_p = _final_norm(xp, final_g, TM_PROMPT).reshape(bp, seq, d)
    y_s = _final_norm(xs, final_g, tms).reshape(bs, 1, d)
    st = lambda name: outs[name][0][None] if len(outs[name]) == 1 else jnp.stack(outs[name])
    return (y_p, y_s, st("rg_conv_p"), st("rg_h_p"), st("ssd_conv_p"), st("ssd_p"),
            st("rg_conv_s"), st("rg_h_s"), st("ssd_conv_s"), st("ssd_s"))
```

```python
import functools
import math

import jax
import jax.numpy as jnp
from jax import lax
from jax.experimental import pallas as pl
from jax.experimental.pallas import tpu as pltpu

F32 = jnp.float32
BF16 = jnp.bfloat16

EPS = 1e-6
CONV_W = 4
RG_BLOCKS = 8
RG_C = 8.0
SSD_HEAD_DIM = 64
SSD_GROUPS = 8
SSD_STATE = 128
SSD_CHUNK = 128
PEER_HEADS = 8
PEER_NKEYS = 128
PEER_TOPK = 16
LANES = 128
SUBLANES = 8
VMEM_LIMIT = 56 * 1024 * 1024
NEG_INF = float("-inf")


def _params(*sem):
    return pltpu.CompilerParams(dimension_semantics=sem, vmem_limit_bytes=VMEM_LIMIT)


def _silu(x):
    return x * jax.nn.sigmoid(x)


def _softplus(x):
    return jnp.maximum(x, 0.0) + jnp.log1p(jnp.exp(-jnp.abs(x)))


def _rms(x):
    return x * lax.rsqrt(jnp.mean(x * x, axis=-1, keepdims=True) + EPS)


def _dot(a, b):
    return jnp.dot(a, b, preferred_element_type=F32)


def _dot_nt(a, b):
    return lax.dot_general(a, b, (((1,), (1,)), ((), ())), preferred_element_type=F32)


def _dot_tn(a, b):
    return lax.dot_general(a, b, (((0,), (0,)), ((), ())), preferred_element_type=F32)


def _mod_kernel(c_ref, w_ref, b_ref, o_ref):
    cs = _silu(c_ref[...]).astype(BF16)
    o_ref[...] = _dot(cs, w_ref[...].astype(BF16)) + b_ref[...]


def _modulation(c_all, w_mod, b_mod, chunk):
    depth, d, n = w_mod.shape
    rows = c_all.shape[0]
    tn = 512
    per = chunk // tn
    return pl.pallas_call(
        _mod_kernel,
        grid=(depth, n // tn),
        in_specs=[
            pl.BlockSpec((rows, d), lambda l, j: (0, 0)),
            pl.BlockSpec((None, d, tn), lambda l, j: (l, 0, j)),
            pl.BlockSpec((None, 1, tn), lambda l, j: (l, 0, j)),
        ],
        out_specs=pl.BlockSpec((None, None, rows, tn), lambda l, j: (l, j // per, 0, j % per)),
        out_shape=jax.ShapeDtypeStruct((depth, n // chunk, rows, chunk), F32),
        compiler_params=_params("arbitrary", "arbitrary"),
        name="modulation",
    )(c_all, w_mod, b_mod.reshape(depth, 1, n))


def _proj_kernel(*refs, pro, epi, emit_lhs, n_extra, out_dtype):
    it = iter(refs)
    if pro == "normmod":
        x_ref, g_ref, sc_ref, sh_ref = next(it), next(it), next(it), next(it)
    elif pro == "gated":
        y_ref, z_ref, g_ref = next(it), next(it), next(it)
    else:
        x_ref = next(it)
    w_ref = next(it)
    b_ref = next(it) if epi in ("bias", "resid") else None
    if epi == "resid":
        xres_ref, gate_ref = next(it), next(it)
    wx_ref = next(it) if n_extra else None
    o_ref = next(it)
    lhs_out_ref = next(it) if emit_lhs else None
    ox_ref = next(it) if n_extra else None
    lhs_ref = next(it) if pro != "plain" else None

    j = pl.program_id(1)

    if pro != "plain":
        @pl.when(j == 0)
        def _():
            if pro == "normmod":
                x = x_ref[...]
                h = _rms(x) * g_ref[...] * (1.0 + sc_ref[...]) + sh_ref[...]
            else:
                y = y_ref[...]
                h = _rms(y * _silu(z_ref[...])) * g_ref[...]
            hb = h.astype(BF16)
            lhs_ref[...] = hb
            if emit_lhs:
                lhs_out_ref[...] = hb
            if n_extra:
                ox_ref[...] = _dot(hb, wx_ref[...])
        lhs = lhs_ref[...]
    else:
        lhs = x_ref[...]

    acc = _dot(lhs, w_ref[...])
    if b_ref is not None:
        acc = acc + b_ref[...]
    if epi == "resid":
        acc = xres_ref[...] + gate_ref[...] * acc
    o_ref[...] = acc.astype(out_dtype)


def _projection(pro, epi, ins, w, *, tm, tn, rows_per_mod=None, bias=None, resid=None,
                w_extra=None, emit_lhs=False, out_dtype=F32, name="proj"):
    k, n = w.shape
    m = ins[0].shape[0]
    assert m % tm == 0 and n % tn == 0
    grid = (m // tm, n // tn)

    def mod_spec(width_block, col_of):
        if rows_per_mod == 1:
            return pl.BlockSpec((tm, width_block), lambda i, j: (i, col_of(j)))
        assert rows_per_mod % tm == 0
        t = rows_per_mod // tm
        return pl.BlockSpec((None, 1, width_block), lambda i, j: (i // t, 0, col_of(j)))

    def mod_arr(a):
        return a if rows_per_mod == 1 else a.reshape(a.shape[0], 1, a.shape[1])

    args, specs = [], []
    if pro == "normmod":
        x, g, sc, sh = ins
        args += [x, g.reshape(1, k), mod_arr(sc), mod_arr(sh)]
        specs += [pl.BlockSpec((tm, k), lambda i, j: (i, 0)),
                  pl.BlockSpec((1, k), lambda i, j: (0, 0)),
                  mod_spec(k, lambda j: 0), mod_spec(k, lambda j: 0)]
    elif pro == "gated":
        y, z, g = ins
        args += [y, z, g.reshape(1, k)]
        specs += [pl.BlockSpec((tm, k), lambda i, j: (i, 0)),
                  pl.BlockSpec((tm, k), lambda i, j: (i, 0)),
                  pl.BlockSpec((1, k), lambda i, j: (0, 0))]
    else:
        (x,) = ins
        args += [x]
        specs += [pl.BlockSpec((tm, k), lambda i, j: (i, 0))]
    args.append(w)
    specs.append(pl.BlockSpec((k, tn), lambda i, j: (0, j)))
    if epi in ("bias", "resid"):
        b = bias if bias is not None else jnp.zeros((n,), F32)
        args.append(b.reshape(1, n))
        specs.append(pl.BlockSpec((1, tn), lambda i, j: (0, j)))
    if epi == "resid":
        xres, gate = resid
        args += [xres, mod_arr(gate)]
        specs += [pl.BlockSpec((tm, tn), lambda i, j: (i, j)), mod_spec(tn, lambda j: j)]
    n_extra = 0
    if w_extra is not None:
        n_extra = w_extra.shape[1]
        args.append(w_extra)
        specs.append(pl.BlockSpec((k, n_extra), lambda i, j: (0, 0)))

    out_shapes = [jax.ShapeDtypeStruct((m, n), out_dtype)]
    out_specs = [pl.BlockSpec((tm, tn), lambda i, j: (i, j))]
    if emit_lhs:
        out_shapes.append(jax.ShapeDtypeStruct((m, k), BF16))
        out_specs.append(pl.BlockSpec((tm, k), lambda i, j: (i, 0)))
    if n_extra:
        out_shapes.append(jax.ShapeDtypeStruct((m, n_extra), F32))
        out_specs.append(pl.BlockSpec((tm, n_extra), lambda i, j: (i, 0)))
    scratch = [pltpu.VMEM((tm, k), BF16)] if pro != "plain" else []

    outs = pl.pallas_call(
        functools.partial(_proj_kernel, pro=pro, epi=epi, emit_lhs=emit_lhs, n_extra=n_extra,
                          out_dtype=out_dtype),
        grid=grid, in_specs=specs, out_specs=out_specs, out_shape=out_shapes,
        scratch_shapes=scratch,
        compiler_params=_params("arbitrary", "arbitrary"),
        name=name,
    )(*args)
    return outs if len(outs) > 1 else outs[0]


def _rglru_gates(xc, wa_ref, ba_ref, wi_ref, bi_ref, lam_ref):
    xcb = xc.astype(BF16)
    r = jax.nn.sigmoid(_dot(xcb, wa_ref[...]) + ba_ref[...])
    i = jax.nn.sigmoid(_dot(xcb, wi_ref[...]) + bi_ref[...])
    log_a = (-RG_C * _softplus(-lam_ref[...])) * r
    a = jnp.exp(log_a)
    mult = jnp.sqrt(jnp.maximum(-jnp.tanh(log_a) * (a * a + 1.0), 0.0))
    return a, mult * (i * xc)


RG_ROWS = 256


def _rglru_seq_kernel(gate_ref, xr_ref, cw_ref, cb_ref, wa_ref, ba_ref, wi_ref, bi_ref, lam_ref,
                      y_ref, hlast_ref, xpad_ref):
    seq, blk = xr_ref.shape
    pad = SUBLANES
    xpad_ref[0:pad, :] = jnp.zeros((pad, blk), F32)
    xpad_ref[pad:pad + seq, :] = xr_ref[...]
    row_in_tile = lax.broadcasted_iota(jnp.int32, (RG_ROWS, blk), 0) % SUBLANES

    def step(c, carry):
        start = pl.multiple_of(c * RG_ROWS, RG_ROWS)
        xe = xpad_ref[pl.ds(start, RG_ROWS + pad), :]
        xc = cb_ref[...] + cw_ref[3:4, :] * xe[pad:]
        for k in range(1, CONV_W):
            xc = xc + cw_ref[3 - k:4 - k, :] * pltpu.roll(xe, k, axis=0)[pad:]
        a, b = _rglru_gates(xc, wa_ref, ba_ref, wi_ref, bi_ref, lam_ref)
        for s in (1, 2, 4):
            m = row_in_tile >= s
            b = jnp.where(m, a * pltpu.roll(b, s, axis=0) + b, b)
            a = jnp.where(m, a * pltpu.roll(a, s, axis=0), a)
        hs = []
        for t in range(RG_ROWS // SUBLANES):
            sl = slice(t * SUBLANES, (t + 1) * SUBLANES)
            h = b[sl] + a[sl] * carry
            hs.append(h)
            carry = h[SUBLANES - 1:SUBLANES, :]
        h_all = jnp.concatenate(hs, axis=0)
        y_ref[pl.ds(start, RG_ROWS), :] = (h_all * jax.nn.gelu(gate_ref[pl.ds(start, RG_ROWS), :])).astype(BF16)
        return carry

    carry = lax.fori_loop(0, seq // RG_ROWS, step, jnp.zeros((1, blk), F32))
    hlast_ref[...] = carry


def _rglru_seq(proj, conv_w, conv_b, w_a, b_a, w_i, b_i, lam):
    nb, seq, two_d = proj.shape
    d = two_d // 2
    blk = d // RG_BLOCKS
    vec = lambda a: a.reshape(1, d)
    vspec = pl.BlockSpec((1, blk), lambda b, k: (0, k))
    wspec = pl.BlockSpec((None, blk, blk), lambda b, k: (k, 0, 0))
    return pl.pallas_call(
        _rglru_seq_kernel,
        grid=(nb, RG_BLOCKS),
        in_specs=[
            pl.BlockSpec((None, seq, blk), lambda b, k: (b, 0, k)),
            pl.BlockSpec((None, seq, blk), lambda b, k: (b, 0, RG_BLOCKS + k)),
            pl.BlockSpec((CONV_W, blk), lambda b, k: (0, k)),
            vspec, wspec, vspec, wspec, vspec, vspec,
        ],
        out_specs=[
            pl.BlockSpec((None, seq, blk), lambda b, k: (b, 0, k)),
            pl.BlockSpec((None, 1, blk), lambda b, k: (b, 0, k)),
        ],
        out_shape=[jax.ShapeDtypeStruct((nb, seq, d), BF16), jax.ShapeDtypeStruct((nb, 1, d), F32)],
        scratch_shapes=[pltpu.VMEM((seq + SUBLANES, blk), F32)],
        compiler_params=_params("arbitrary", "arbitrary"),
        name="rglru_seq",
    )(proj, proj, conv_w, vec(conv_b), w_a.astype(BF16), vec(b_a), w_i.astype(BF16), vec(b_i), vec(lam))


def _rglru_step_kernel(gate_ref, xr_ref, c0_ref, c1_ref, c2_ref, h0_ref, cw_ref, cb_ref,
                       wa_ref, ba_ref, wi_ref, bi_ref, lam_ref, y_ref, h_ref):
    xc = (cb_ref[...] + cw_ref[0:1, :] * c0_ref[...] + cw_ref[1:2, :] * c1_ref[...]
          + cw_ref[2:3, :] * c2_ref[...] + cw_ref[3:4, :] * xr_ref[...])
    a, b = _rglru_gates(xc, wa_ref, ba_ref, wi_ref, bi_ref, lam_ref)
    h = a * h0_ref[...] + b
    h_ref[...] = h
    y_ref[...] = (h * jax.nn.gelu(gate_ref[...])).astype(BF16)


def _rglru_step(proj, conv_state, h0, conv_w, conv_b, w_a, b_a, w_i, b_i, lam):
    nb, two_d = proj.shape
    d = two_d // 2
    blk = d // RG_BLOCKS
    vec = lambda a: a.reshape(1, d)
    rows = pl.BlockSpec((nb, blk), lambda k: (0, k))
    vspec = pl.BlockSpec((1, blk), lambda k: (0, k))
    wspec = pl.BlockSpec((None, blk, blk), lambda k: (k, 0, 0))
    return pl.pallas_call(
        _rglru_step_kernel,
        grid=(RG_BLOCKS,),
        in_specs=[rows, pl.BlockSpec((nb, blk), lambda k: (0, RG_BLOCKS + k)), rows, rows, rows, rows,
                  pl.BlockSpec((CONV_W, blk), lambda k: (0, k)),
                  vspec, wspec, vspec, wspec, vspec, vspec],
        out_specs=[rows, rows],
        out_shape=[jax.ShapeDtypeStruct((nb, d), BF16), jax.ShapeDtypeStruct((nb, d), F32)],
        compiler_params=_params("arbitrary"),
        name="rglru_step",
    )(proj, proj, conv_state[:, 0], conv_state[:, 1], conv_state[:, 2], h0, conv_w, vec(conv_b),
      w_a.astype(BF16), vec(b_a), w_i.astype(BF16), vec(b_i), vec(lam))


def _split3(x):
    hi = x.astype(BF16)
    r1 = x - hi.astype(F32)
    mid = r1.astype(BF16)
    lo = (r1 - mid.astype(F32)).astype(BF16)
    return hi, mid, lo


def _head_expand(x, expand_ref):
    hi = x.astype(BF16)
    lo = (x - hi.astype(F32)).astype(BF16)
    return _dot(hi, expand_ref[...]) + _dot(lo, expand_ref[...])


def _ssd_seq_kernel(xbc_ref, halo_ref, dt_ref, z_ref, cw_ref, cb_ref, dtb_ref, alog_ref, drep_ref, g_ref,
                    expand_ref, yn_ref, st_ref, *, d_inner):
    c = pl.program_id(1)
    q = SSD_CHUNK
    p = SSD_HEAD_DIM
    n = SSD_STATE
    hpg = d_inner // p // SSD_GROUPS
    epp = hpg * p
    pad = SUBLANES

    @pl.when(c == 0)
    def _():
        st_ref[...] = jnp.zeros(st_ref.shape, F32)

    halo = jnp.where(c > 0, halo_ref[...], 0.0)
    xe = jnp.concatenate([halo, xbc_ref[...]], axis=0)
    xc = cb_ref[...] + cw_ref[3:4, :] * xe[pad:]
    for k in range(1, CONV_W):
        xc = xc + cw_ref[3 - k:4 - k, :] * pltpu.roll(xe, k, axis=0)[pad:]
    xc = _silu(xc)
    xs = xc[:, :d_inner]

    dt = _softplus(dt_ref[...] + dtb_ref[...])
    da = dt * (-jnp.exp(alog_ref[...]))
    ri = lax.broadcasted_iota(jnp.int32, (q, q), 0)
    ci = lax.broadcasted_iota(jnp.int32, (q, q), 1)
    tri = ri >= ci
    ltri = jnp.where(tri, 1.0, 0.0).astype(BF16)
    hi, mid, lo = _split3(da)
    cum = _dot(ltri, hi) + _dot(ltri, mid) + _dot(ltri, lo)
    cum_t = cum.T
    dt_t = dt.T
    last = cum[q - 1:q, :]
    ecum_x = _head_expand(jnp.exp(cum), expand_ref)
    wend_x = _head_expand(jnp.exp(last - cum) * dt, expand_ref)
    elast_x = _head_expand(jnp.broadcast_to(jnp.exp(last), (SUBLANES, last.shape[1])), expand_ref)[0:1, :]
    xw = (xs * wend_x).astype(BF16)
    first_head = lax.broadcasted_iota(jnp.int32, (q, 2 * p), 1) < p

    y_groups = []
    for g in range(SSD_GROUPS):
        bb = xc[:, d_inner + g * n:d_inner + (g + 1) * n].astype(BF16)
        cc = xc[:, d_inner + SSD_GROUPS * n + g * n:d_inner + SSD_GROUPS * n + (g + 1) * n].astype(BF16)
        cb = _dot_nt(cc, bb)
        st = st_ref[g]
        ys = _dot(cc, st.astype(BF16))
        pairs = []
        for e in range(0, hpg, 2):
            xpair = xs[:, (g * hpg + e) * p:(g * hpg + e + 2) * p].astype(BF16)
            acc = None
            for k in range(2):
                h = g * hpg + e + k
                decay = jnp.exp(jnp.where(tri, cum[:, h:h + 1] - cum_t[h:h + 1, :], NEG_INF))
                w = (decay * cb * dt_t[h:h + 1, :]).astype(BF16)
                keep = first_head if k == 0 else jnp.logical_not(first_head)
                term = _dot(w, jnp.where(keep, xpair, jnp.zeros((), BF16)))
                acc = term if acc is None else acc + term
            pairs.append(acc)
        lanes = slice(g * epp, (g + 1) * epp)
        y_groups.append(jnp.concatenate(pairs, axis=1) + ys * ecum_x[:, lanes])
        st_ref[g] = st * elast_x[:, lanes] + _dot_tn(bb, xw[:, lanes])
    y = jnp.concatenate(y_groups, axis=1) + drep_ref[...] * xs
    yn_ref[...] = (_rms(y * _silu(z_ref[...])) * g_ref[...]).astype(BF16)


def _ssd_seq(xbc, dt_raw, z, conv_w, conv_b, dt_bias, a_log, d_skip, norm_g, d_inner):
    nb, seq, cdim = xbc.shape
    q = SSD_CHUNK
    hp = dt_raw.shape[-1]
    heads = d_inner // SSD_HEAD_DIM
    epp = d_inner // SSD_GROUPS
    padv = lambda a: jnp.pad(a.reshape(1, heads), ((0, 0), (0, hp - heads)))
    halo_blocks = q // SUBLANES
    expand = (jnp.arange(d_inner)[None, :] // SSD_HEAD_DIM == jnp.arange(hp)[:, None]).astype(BF16)
    const = lambda shape: pl.BlockSpec(shape, lambda b, c: (0, 0))
    return pl.pallas_call(
        functools.partial(_ssd_seq_kernel, d_inner=d_inner),
        grid=(nb, seq // q),
        in_specs=[
            pl.BlockSpec((None, q, cdim), lambda b, c: (b, c, 0)),
            pl.BlockSpec((None, SUBLANES, cdim), lambda b, c: (b, jnp.maximum(c * halo_blocks - 1, 0), 0)),
            pl.BlockSpec((None, q, hp), lambda b, c: (b, c, 0)),
            pl.BlockSpec((None, q, d_inner), lambda b, c: (b, c, 0)),
            const((CONV_W, cdim)), const((1, cdim)), const((1, hp)), const((1, hp)),
            const((1, d_inner)), const((1, d_inner)), const((hp, d_inner)),
        ],
        out_specs=[
            pl.BlockSpec((None, q, d_inner), lambda b, c: (b, c, 0)),
            pl.BlockSpec((None, SSD_GROUPS, SSD_STATE, epp), lambda b, c: (b, 0, 0, 0)),
        ],
        out_shape=[jax.ShapeDtypeStruct((nb, seq, d_inner), BF16),
                   jax.ShapeDtypeStruct((nb, SSD_GROUPS, SSD_STATE, epp), F32)],
        compiler_params=_params("arbitrary", "arbitrary"),
        name="ssd_seq",
    )(xbc, xbc, dt_raw, z, conv_w, conv_b.reshape(1, cdim), padv(dt_bias), padv(a_log),
      jnp.repeat(d_skip, SSD_HEAD_DIM).reshape(1, d_inner), norm_g.reshape(1, d_inner), expand)


def _ssd_step_pre_kernel(xbc_ref, c0_ref, c1_ref, c2_ref, dt_ref, cw_ref, cb_ref, dtb_ref, alog_ref, drep_ref,
                         xdt_ref, skip_ref, bc_ref, dec_ref, *, d_inner):
    xc = (cb_ref[...] + cw_ref[0:1, :] * c0_ref[...] + cw_ref[1:2, :] * c1_ref[...]
          + cw_ref[2:3, :] * c2_ref[...] + cw_ref[3:4, :] * xbc_ref[...])
    xc = _silu(xc)
    dt = _softplus(dt_ref[...] + dtb_ref[...])
    dec_ref[...] = jnp.exp(dt * (-jnp.exp(alog_ref[...])))
    hp = dt.shape[1]
    head_of_lane = lax.broadcasted_iota(jnp.int32, (hp, d_inner), 1) // SSD_HEAD_DIM
    expand = jnp.where(head_of_lane == lax.broadcasted_iota(jnp.int32, (hp, d_inner), 0), 1.0, 0.0).astype(BF16)
    dt_rep = sum(_dot(piece, expand) for piece in _split3(dt))
    xs = xc[:, :d_inner]
    xdt_ref[...] = xs * dt_rep
    skip_ref[...] = xs * drep_ref[...]
    bc_ref[...] = xc[:, d_inner:]


def _ssd_step_state_kernel(dec_ref, s_ref, xdt_ref, skip_ref, b_ref, c_ref, s_out_ref, y_ref):
    tb, npair, rows, n = s_ref.shape
    pairs_per_group = npair // SSD_GROUPS
    seq0 = pl.program_id(0) * tb
    row_id = lax.broadcasted_iota(jnp.int32, (rows, n), 0)
    pair_id = lax.broadcasted_iota(jnp.int32, (npair, n), 0)
    for b in range(tb):
        xt = xdt_ref[b].T.astype(BF16)
        bm = b_ref[b]
        brep = jnp.concatenate([jnp.broadcast_to(bm[g:g + 1], (pairs_per_group, n))
                                for g in range(SSD_GROUPS)], axis=0)
        bbig = jnp.concatenate([jnp.where(pair_id == j, brep, 0.0) for j in range(npair)], axis=1)
        upd = _dot(xt, bbig.astype(BF16))
        cm = c_ref[b].astype(BF16)
        for j in range(npair):
            g = j // pairs_per_group
            d0 = dec_ref[seq0 + b, 2 * j]
            d1 = dec_ref[seq0 + b, 2 * j + 1]
            s_new = s_ref[b, j] * jnp.where(row_id < rows // 2, d0, d1) + upd[:, j * n:(j + 1) * n]
            s_out_ref[b, j] = s_new
            y8 = _dot_nt(cm, s_new.astype(BF16))
            y_ref[b, j:j + 1, :] = y8[g:g + 1, :] + skip_ref[b, j:j + 1, :]


def _ssd_step(xbc_raw, conv_state, dt_raw, state, conv_w, conv_b, dt_bias, a_log, d_skip, d_inner):
    nb, cdim = xbc_raw.shape
    hp = dt_raw.shape[-1]
    heads = d_inner // SSD_HEAD_DIM
    p, n = SSD_HEAD_DIM, SSD_STATE
    gn = SSD_GROUPS * n
    padv = lambda a: jnp.pad(a.reshape(1, heads), ((0, 0), (0, hp - heads)))
    full = lambda shape: pl.BlockSpec(shape, lambda: tuple(0 for _ in shape))
    xdt, skip, bc, dec = pl.pallas_call(
        functools.partial(_ssd_step_pre_kernel, d_inner=d_inner),
        in_specs=[full((nb, cdim))] * 4 + [full((nb, hp)), full((CONV_W, cdim)), full((1, cdim)),
                                            full((1, hp)), full((1, hp)), full((1, d_inner))],
        out_specs=[full((nb, d_inner)), full((nb, d_inner)), full((nb, 2 * gn)), full((nb, hp))],
        out_shape=[jax.ShapeDtypeStruct((nb, d_inner), F32), jax.ShapeDtypeStruct((nb, d_inner), F32),
                   jax.ShapeDtypeStruct((nb, 2 * gn), F32), jax.ShapeDtypeStruct((nb, hp), F32)],
        compiler_params=pltpu.CompilerParams(vmem_limit_bytes=VMEM_LIMIT),
        name="ssd_step_pre",
    )(xbc_raw, conv_state[:, 0], conv_state[:, 1], conv_state[:, 2], dt_raw, conv_w,
      conv_b.reshape(1, cdim), padv(dt_bias), padv(a_log), jnp.repeat(d_skip, p).reshape(1, d_inner))
    npair = heads // 2
    pair3 = lambda a: a.reshape(nb, npair, 2 * p)
    tb = 2
    sspec = pl.BlockSpec((tb, npair, 2 * p, n), lambda i: (i, 0, 0, 0))
    vspec = pl.BlockSpec((tb, npair, 2 * p), lambda i: (i, 0, 0))
    gspec = pl.BlockSpec((tb, SSD_GROUPS, n), lambda i: (i, 0, 0))
    s_new, y3 = pl.pallas_call(
        _ssd_step_state_kernel,
        grid=(nb // tb,),
        in_specs=[pl.BlockSpec(memory_space=pltpu.SMEM), sspec, vspec, vspec, gspec, gspec],
        out_specs=[sspec, vspec],
        out_shape=[jax.ShapeDtypeStruct((nb, npair, 2 * p, n), F32),
                   jax.ShapeDtypeStruct((nb, npair, 2 * p), F32)],
        compiler_params=_params("arbitrary"),
        name="ssd_step_state",
    )(dec, state.reshape(nb, npair, 2 * p, n), pair3(xdt), pair3(skip),
      bc[:, :gn].reshape(nb, SSD_GROUPS, n), bc[:, gn:].reshape(nb, SSD_GROUPS, n))
    return y3.reshape(nb, d_inner), s_new


NOT_RANKED = 99.0


def _tree(op, xs):
    xs = list(xs)
    while len(xs) > 1:
        xs = [op(xs[i], xs[i + 1]) if i + 1 < len(xs) else xs[i] for i in range(0, len(xs), 2)]
    return xs[0]
PEER_PAIRS = [(a, b) for a in range(PEER_TOPK) for b in range(PEER_TOPK) if (a + 1) * (b + 1) <= PEER_TOPK]


def _sort_network(n):
    pairs = []
    p = 1
    while p < n:
        k = p
        while k >= 1:
            for j in range(k % p, n - k, 2 * k):
                for i in range(min(k, n - j - k)):
                    if (i + j) // (2 * p) == (i + j + k) // (2 * p):
                        pairs.append((i + j, i + j + k))
            k //= 2
        p *= 2
    return pairs


SORT16 = _sort_network(PEER_TOPK)
BITONIC16 = [(i, i + st) for st in (8, 4, 2, 1) for i in range(PEER_TOPK) if (i // st) % 2 == 0]


def _exchange(xs, pairs):
    for i, j in pairs:
        xs[i], xs[j] = jnp.maximum(xs[i], xs[j]), jnp.minimum(xs[i], xs[j])
    return xs


def _top16_values(w):
    groups = [_exchange(list(w[g:g + PEER_TOPK]), SORT16) for g in range(0, len(w), PEER_TOPK)]
    dropped = []
    while len(groups) > 1:
        merged = []
        for a, b in zip(groups[0::2], groups[1::2]):
            hi = [jnp.maximum(a[i], b[PEER_TOPK - 1 - i]) for i in range(PEER_TOPK)]
            dropped.append(_tree(jnp.maximum, [jnp.minimum(a[i], b[PEER_TOPK - 1 - i]) for i in range(PEER_TOPK)]))
            merged.append(_exchange(hi, BITONIC16))
        groups = merged
    return groups[0], _tree(jnp.maximum, dropped)


def _staircase(v1, v2, shape):
    cand = {c: v1[c[0]] + v2[c[1]] for c in PEER_PAIRS}
    ahead_of = {c: [] for c in PEER_PAIRS}
    behind_of = {c: [] for c in PEER_PAIRS}
    for n1, c in enumerate(PEER_PAIRS):
        for c2 in PEER_PAIRS[n1 + 1:]:
            if c2[0] >= c[0] and c2[1] >= c[1]:
                continue
            ahead = jnp.where(cand[c] >= cand[c2], 1.0, 0.0)
            ahead_of[c2].append(ahead)
            behind_of[c].append(ahead)
    e1v = [jnp.exp(v1[a] - v1[0]) for a in range(PEER_TOPK)]
    e2v = [jnp.exp(v2[b] - v2[0]) for b in range(PEER_TOPK)]
    keep_terms = [[] for _ in range(PEER_TOPK)]
    z_terms = []
    for c in PEER_PAIRS:
        cnt = jnp.full(shape, float((c[0] + 1) * (c[1] + 1) - 1 + len(behind_of[c])), F32)
        if ahead_of[c]:
            cnt = cnt + _tree(jnp.add, ahead_of[c])
        if behind_of[c]:
            cnt = cnt - _tree(jnp.add, behind_of[c])
        keep = jnp.where(cnt < float(PEER_TOPK), 1.0, 0.0)
        keep_terms[c[0]].append(keep)
        z_terms.append(keep * (e1v[c[0]] * e2v[c[1]]))
    return [_tree(jnp.add, t) for t in keep_terms], 1.0 / _tree(jnp.add, z_terms)


def _peer_prep_kernel(q_ref, k1_ref, k2_ref, r2_ref, e2_ref, c1_ref, e1_ref,
                      s_scr, work_scr, rank_scr, v_scr):
    tt = q_ref.shape[0]
    hh = PEER_HEADS
    half = k1_ref.shape[1]
    tile = lambda k: slice(k * hh, (k + 1) * hh)
    twice = lambda a: jnp.concatenate([a, a], axis=0)

    for side, k_ref in enumerate((k1_ref, k2_ref)):
        s_scr[side] = _dot_nt(k_ref[...], q_ref[:, side * half:(side + 1) * half])

    tops, ties = [], []
    for side in range(2):
        v, nxt = _top16_values([s_scr[side, tile(k), :] for k in range(PEER_NKEYS)])
        tops.append(v)
        ties += [jnp.where(v[a] == v[a + 1], 1.0, 0.0) for a in range(PEER_TOPK - 1)]
        ties.append(jnp.where(v[PEER_TOPK - 1] == nxt, 1.0, 0.0))
    v1, v2 = tops
    ncols, rz = _staircase(v1, v2, (hh, tt))
    inf = jnp.full((hh, tt), float("inf"), F32)
    thr = [_tree(jnp.minimum, [jnp.where(ncols[a] > float(m), v1[a], inf)
                               for a in range(PEER_TOPK) if (a + 1) * (m + 1) <= PEER_TOPK])
           for m in range(PEER_TOPK)]
    thr2 = [twice(t) for t in thr]
    v2x = [twice(v) for v in v2]
    m1, m2, rz2 = twice(v1[0]), twice(v2[0]), twice(rz)
    for k in range(0, PEER_NKEYS, 2):
        rows = slice(k * hh, (k + 2) * hh)
        s1 = s_scr[0, rows, :]
        s2 = s_scr[1, rows, :]
        c1_ref[rows, :] = _tree(jnp.add, [jnp.where(s1 >= t, 1.0, 0.0) for t in thr2])
        r2_ref[rows, :] = _tree(jnp.add, [jnp.where(v > s2, 1.0, 0.0) for v in v2x]).astype(BF16)
        e1_ref[rows, :] = jnp.exp(s1 - m1) * rz2
        e2_ref[rows, :] = jnp.exp(s2 - m2).astype(BF16)

    @pl.when(jnp.max(_tree(jnp.maximum, ties)) > 0.0)
    def _():
        for side in range(2):
            work_scr[side] = s_scr[side]
            rank_scr[side] = jnp.full(work_scr.shape[1:], NOT_RANKED, F32)

        def extract(a, _):
            af = jnp.asarray(a, F32)
            for side in range(2):
                w = [work_scr[side, tile(k), :] for k in range(PEER_NKEYS)]
                m = _tree(jnp.maximum, w)
                sel = _tree(jnp.minimum, [jnp.where(w[k] == m, k, PEER_NKEYS) for k in range(PEER_NKEYS)])
                for k in range(PEER_NKEYS):
                    hit = sel == k
                    work_scr[side, tile(k), :] = jnp.where(hit, NEG_INF, w[k])
                    rank_scr[side, tile(k), :] = jnp.where(hit, af, rank_scr[side, tile(k), :])
                v_scr[side, a] = m
            return 0

        lax.fori_loop(0, PEER_TOPK, extract, 0)
        u1 = [v_scr[0, a] for a in range(PEER_TOPK)]
        u2 = [v_scr[1, b] for b in range(PEER_TOPK)]
        ncols_t, rz_t = _staircase(u1, u2, (hh, tt))
        ncols2 = [twice(c) for c in ncols_t]
        rzt2 = twice(rz_t)
        for k in range(0, PEER_NKEYS, 2):
            rows = slice(k * hh, (k + 2) * hh)
            r1 = rank_scr[0, rows, :]
            c1 = jnp.zeros((2 * hh, tt), F32)
            for a in range(PEER_TOPK):
                c1 = jnp.where(r1 == float(a), ncols2[a], c1)
            c1_ref[rows, :] = c1
            e1_ref[rows, :] = jnp.exp(s_scr[0, rows, :] - m1) * rzt2
            r2_ref[rows, :] = rank_scr[1, rows, :].astype(BF16)


def _peer_prep(qp, k1big, k2big, tt=LANES):
    t = qp.shape[0]
    rows = k1big.shape[0]
    out = lambda dt: jax.ShapeDtypeStruct((rows, t), dt)
    ospec = pl.BlockSpec((rows, tt), lambda i: (0, i))
    return pl.pallas_call(
        _peer_prep_kernel,
        grid=(t // tt,),
        in_specs=[pl.BlockSpec((tt, qp.shape[1]), lambda i: (i, 0)),
                  pl.BlockSpec(k1big.shape, lambda i: (0, 0)),
                  pl.BlockSpec(k2big.shape, lambda i: (0, 0))],
        out_specs=[ospec] * 4,
        out_shape=[out(BF16), out(BF16), out(F32), out(F32)],
        scratch_shapes=[pltpu.VMEM((2, rows, tt), F32), pltpu.VMEM((2, rows, tt), F32),
                        pltpu.VMEM((2, rows, tt), F32), pltpu.VMEM((2, PEER_TOPK, PEER_HEADS, tt), F32)],
        compiler_params=_params("arbitrary"),
        name="peer_prep",
    )(qp, k1big, k2big)


def _peer_expert_kernel(hc_ref, u_ref, vt_ref, r2_ref, e2_ref, c1_ref, e1_ref, x_ref, gate_ref,
                        o_ref, acc_scr):
    e = pl.program_id(1)
    te = u_ref.shape[0]
    nk = PEER_NKEYS

    @pl.when(e == 0)
    def _():
        acc_scr[...] = jnp.zeros(acc_scr.shape, F32)

    act = jax.nn.gelu(_dot_nt(u_ref[...], hc_ref[...])).astype(BF16)
    blocks = []
    for ii in range(te // nk):
        w = None
        for h in range(PEER_HEADS):
            row = ii * PEER_HEADS + h
            c1 = c1_ref[row:row + 1, :].astype(BF16)
            e1 = e1_ref[row:row + 1, :].astype(BF16)
            term = jnp.where(r2_ref[h] < c1, e2_ref[h], jnp.zeros((), BF16)) * e1
            w = term if w is None else w + term
        blocks.append(act[ii * nk:(ii + 1) * nk] * w)
    acc_scr[...] += _dot(vt_ref[...], jnp.concatenate(blocks, axis=0))

    @pl.when(e == pl.num_programs(1) - 1)
    def _():
        o_ref[...] = x_ref[...] + gate_ref[...] * acc_scr[...].T


def _peer_expert(hc, u_bf, vt_bf, layer, r2, e2, c1, e1, x, gate, *, tm, te, rows_per_mod):
    t, d = hc.shape
    n_exp = u_bf.shape[1]
    rows_i = te // PEER_NKEYS * PEER_HEADS
    hh, nk = PEER_HEADS, PEER_NKEYS
    once = pl.Buffered(1)
    if rows_per_mod == 1:
        gate_arr = gate
        gate_spec = pl.BlockSpec((tm, d), lambda i, e: (i, 0), pipeline_mode=once)
    else:
        per = rows_per_mod // tm
        gate_arr = gate.reshape(gate.shape[0], 1, d)
        gate_spec = pl.BlockSpec((None, 1, d), lambda i, e: (i // per, 0, 0))
    return pl.pallas_call(
        _peer_expert_kernel,
        grid=(t // tm, n_exp // te),
        in_specs=[
            pl.BlockSpec((tm, d), lambda i, e: (i, 0), pipeline_mode=once),
            pl.BlockSpec((None, te, d), lambda i, e: (layer, e, 0)),
            pl.BlockSpec((None, d, te), lambda i, e: (layer, 0, e)),
            pl.BlockSpec((hh, nk, tm), lambda i, e: (0, 0, i), pipeline_mode=once),
            pl.BlockSpec((hh, nk, tm), lambda i, e: (0, 0, i), pipeline_mode=once),
            pl.BlockSpec((rows_i, tm), lambda i, e: (e, i)),
            pl.BlockSpec((rows_i, tm), lambda i, e: (e, i)),
            pl.BlockSpec((tm, d), lambda i, e: (i, 0), pipeline_mode=once),
            gate_spec,
        ],
        out_specs=pl.BlockSpec((tm, d), lambda i, e: (i, 0)),
        out_shape=jax.ShapeDtypeStruct((t, d), F32),
        scratch_shapes=[pltpu.VMEM((d, tm), F32)],
        compiler_params=_params("arbitrary", "arbitrary"),
        name="peer_expert",
    )(hc, u_bf, vt_bf, r2, e2, c1, e1, x, gate_arr)


def _final_norm_kernel(x_ref, g_ref, o_ref):
    o_ref[...] = _rms(x_ref[...]) * g_ref[...]


def _final_norm(x, g, tm):
    t, d = x.shape
    return pl.pallas_call(
        _final_norm_kernel,
        grid=(t // tm,),
        in_specs=[pl.BlockSpec((tm, d), lambda i: (i, 0)), pl.BlockSpec((1, d), lambda i: (0, 0))],
        out_specs=pl.BlockSpec((tm, d), lambda i: (i, 0)),
        out_shape=jax.ShapeDtypeStruct((t, d), F32),
        compiler_params=_params("arbitrary"),
        name="final_norm",
    )(x, g.reshape(1, d))


TM_PROMPT = 1024
TN_PROJ = 1024
TM_EXPERT = 512
TE_EXPERT = 1024


def _peer_layer(groups, w_q, k1, k2, u_bf, vt_bf, layer, norm_g):
    d = w_q.shape[0]
    hh, nk = PEER_HEADS, PEER_NKEYS
    half = k1.shape[1]
    wq = w_q.reshape(d, hh, 2, half).transpose(0, 2, 1, 3).reshape(d, 2 * hh * half).astype(BF16)
    eye = jnp.eye(hh, dtype=F32)
    kbig = lambda k: jnp.einsum("kd,hg->khgd", k, eye).reshape(nk * hh, hh * half).astype(BF16)
    k1b, k2b = kbig(k1), kbig(k2)
    outs = []
    for gr in groups:
        x = gr["x"]
        t = x.shape[0]
        qp, hc = _projection("normmod", "none", (x, norm_g, gr["sc"], gr["sh"]), wq,
                             tm=gr["tm"], tn=TN_PROJ, rows_per_mod=gr["rows_per_mod"],
                             emit_lhs=True, out_dtype=BF16, name="peer_q")
        r2, e2, c1, e1 = _peer_prep(qp, k1b, k2b)
        r2, e2 = [a.reshape(nk, hh, t).transpose(1, 0, 2) for a in (r2, e2)]
        outs.append(_peer_expert(hc, u_bf, vt_bf, layer, r2, e2, c1, e1, x, gr["gate"],
                                 tm=gr["tme"], te=TE_EXPERT, rows_per_mod=gr["rows_per_mod"]))
    return outs


def kernel(x_prompt, x_sample, state_rg_conv, state_rg_h, state_ssd_conv, state_ssd, c_prompt, c_sample, norm1_g, norm2_g, w_mod, b_mod, rg_w_in, rg_b_in, rg_conv_w, rg_conv_b, rg_w_a, rg_b_a, rg_w_i, rg_b_i, rg_lambda, rg_w_out, rg_b_out, ssd_w_in, ssd_conv_w, ssd_conv_b, ssd_dt_bias, ssd_a_log, ssd_d, ssd_norm_g, ssd_w_out, peer_w_q, peer_k1, peer_k2, peer_u, peer_v, final_g):
    bp, seq, d = x_prompt.shape
    bs = x_sample.shape[0]
    tp = bp * seq
    depth = w_mod.shape[0]
    d_rnn = rg_w_out.shape[1]
    d_inner = ssd_w_out.shape[1]
    cdim = ssd_conv_w.shape[-1]
    heads = d_inner // SSD_HEAD_DIM

    nc = bp + bs
    c_all = jnp.concatenate([c_sample, c_prompt, jnp.zeros((-nc % SUBLANES, d), F32)], axis=0)
    mod = _modulation(c_all, w_mod, b_mod, d)
    u_bf = peer_u.astype(BF16)
    vt_bf = jnp.swapaxes(peer_v, 1, 2).astype(BF16)

    xp = x_prompt.reshape(tp, d)
    xs = x_sample.reshape(bs, d)
    tms = bs
    outs = {}

    for l in range(depth):
        mp = [mod[l, i, bs:nc] for i in range(6)]
        ms = [mod[l, i, :bs] for i in range(6)]
        j = l // 2
        if l % 2 == 0:
            w_in = rg_w_in[j].astype(BF16)
            w_out = rg_w_out[j].astype(BF16)
            rg = (rg_conv_w[j], rg_conv_b[j], rg_w_a[j], rg_b_a[j], rg_w_i[j], rg_b_i[j], rg_lambda[j])
            proj_p = _projection("normmod", "bias", (xp, norm1_g[l], mp[1], mp[0]), w_in, tm=TM_PROMPT,
                                 tn=TN_PROJ, rows_per_mod=seq, bias=rg_b_in[j], name="rg_in")
            proj_s = _projection("normmod", "bias", (xs, norm1_g[l], ms[1], ms[0]), w_in, tm=tms,
                                 tn=TN_PROJ, rows_per_mod=1, bias=rg_b_in[j], name="rg_in")
            proj_p3 = proj_p.reshape(bp, seq, 2 * d_rnn)
            y_p, h_p = _rglru_seq(proj_p3, *rg)
            y_s, h_s = _rglru_step(proj_s, state_rg_conv[j], state_rg_h[j], *rg)
            xp = _projection("plain", "resid", (y_p.reshape(tp, d_rnn),), w_out, tm=TM_PROMPT, tn=TN_PROJ,
                             rows_per_mod=seq, bias=rg_b_out[j], resid=(xp, mp[2]), name="rg_out")
            xs = _projection("plain", "resid", (y_s,), w_out, tm=tms, tn=TN_PROJ,
                             rows_per_mod=1, bias=rg_b_out[j], resid=(xs, ms[2]), name="rg_out")
            outs.setdefault("rg_conv_p", []).append(proj_p3[:, seq - (CONV_W - 1):, d_rnn:])
            outs.setdefault("rg_h_p", []).append(h_p.reshape(bp, d_rnn))
            outs.setdefault("rg_conv_s", []).append(
                jnp.concatenate([state_rg_conv[j][:, 1:], proj_s[:, None, d_rnn:]], axis=1))
            outs.setdefault("rg_h_s", []).append(h_s)
        else:
            w_in = ssd_w_in[j]
            w_z = w_in[:, :d_inner].astype(BF16)
            w_xbc = w_in[:, d_inner:d_inner + cdim].astype(BF16)
            w_dt = jnp.pad(w_in[:, d_inner + cdim:], ((0, 0), (0, LANES - heads))).astype(BF16)
            w_out = ssd_w_out[j].astype(BF16)
            sp = (ssd_conv_w[j], ssd_conv_b[j], ssd_dt_bias[j], ssd_a_log[j], ssd_d[j])
            z_p, hm_p, dt_p = _projection("normmod", "none", (xp, norm1_g[l], mp[1], mp[0]), w_z, tm=TM_PROMPT,
                                          tn=TN_PROJ, rows_per_mod=seq, w_extra=w_dt, emit_lhs=True, name="ssd_in_z")
            z_s, hm_s, dt_s = _projection("normmod", "none", (xs, norm1_g[l], ms[1], ms[0]), w_z, tm=tms,
                                          tn=TN_PROJ, rows_per_mod=1, w_extra=w_dt, emit_lhs=True, name="ssd_in_z")
            xbc_p = _projection("plain", "none", (hm_p,), w_xbc, tm=TM_PROMPT, tn=TN_PROJ, name="ssd_in_xbc")
            xbc_s = _projection("plain", "none", (hm_s,), w_xbc, tm=tms, tn=TN_PROJ, name="ssd_in_xbc")
            xbc_p3 = xbc_p.reshape(bp, seq, cdim)
            yn_p, st_p = _ssd_seq(xbc_p3, dt_p.reshape(bp, seq, LANES), z_p.reshape(bp, seq, d_inner), *sp,
                                  ssd_norm_g[j], d_inner)
            y_s, st_s = _ssd_step(xbc_s, state_ssd_conv[j], dt_s, state_ssd[j], *sp, d_inner)
            xp = _projection("plain", "resid", (yn_p.reshape(tp, d_inner),), w_out, tm=TM_PROMPT, tn=TN_PROJ,
                             rows_per_mod=seq, resid=(xp, mp[2]), name="ssd_out")
            xs = _projection("gated", "resid", (y_s, z_s, ssd_norm_g[j]), w_out, tm=tms, tn=512,
                             rows_per_mod=1, resid=(xs, ms[2]), name="ssd_out")
            g_, n_ = SSD_GROUPS, SSD_STATE
            outs.setdefault("ssd_conv_p", []).append(xbc_p3[:, seq - (CONV_W - 1):])
            outs.setdefault("ssd_p", []).append(
                st_p.reshape(bp, g_, n_, heads // g_, SSD_HEAD_DIM).transpose(0, 1, 3, 4, 2))
            outs.setdefault("ssd_conv_s", []).append(
                jnp.concatenate([state_ssd_conv[j][:, 1:], xbc_s[:, None]], axis=1))
            outs.setdefault("ssd_s", []).append(st_s.reshape(state_ssd[j].shape))

        xp, xs = _peer_layer(
            [dict(x=xp, sc=mp[4], sh=mp[3], gate=mp[5], rows_per_mod=seq, tm=TM_PROMPT, tme=TM_EXPERT),
             dict(x=xs, sc=ms[4], sh=ms[3], gate=ms[5], rows_per_mod=1, tm=tms, tme=tms)],
            peer_w_q[l], peer_k1[l], peer_k2[l], u_bf, vt_bf, l, norm2_g[l])

    y_p = _final_norm(xp, final_g, TM_PROMPT).reshape(bp, seq, d)
    y_s = _final_norm(xs, final_g, tms).reshape(bs, 1, d)
    st = lambda name: outs[name][0][None] if len(outs[name]) == 1 else jnp.stack(outs[name])
    return (y_p, y_s, st("rg_conv_p"), st("rg_h_p"), st("ssd_conv_p"), st("ssd_p"),
            st("rg_conv_s"), st("rg_h_s"), st("ssd_conv_s"), st("ssd_s"))
```

```python
import functools
import math

import jax
import jax.numpy as jnp
from jax import lax
from jax.experimental import pallas as pl
from jax.experimental.pallas import tpu as pltpu

F32 = jnp.float32
BF16 = jnp.bfloat16

EPS = 1e-6
CONV_W = 4
RG_BLOCKS = 8
RG_C = 8.0
SSD_HEAD_DIM = 64
SSD_GROUPS = 8
SSD_STATE = 128
SSD_CHUNK = 128
PEER_HEADS = 8
PEER_NKEYS = 128
PEER_TOPK = 16
LANES = 128
SUBLANES = 8
VMEM_LIMIT = 56 * 1024 * 1024
NEG_INF = float("-inf")


def _params(*sem):
    return pltpu.CompilerParams(dimension_semantics=sem, vmem_limit_bytes=VMEM_LIMIT)


def _silu(x):
    return x * jax.nn.sigmoid(x)


def _softplus(x):
    return jnp.maximum(x, 0.0) + jnp.log1p(jnp.exp(-jnp.abs(x)))


def _rms(x):
    return x * lax.rsqrt(jnp.mean(x * x, axis=-1, keepdims=True) + EPS)


def _dot(a, b):
    return jnp.dot(a, b, preferred_element_type=F32)


def _dot_nt(a, b):
    return lax.dot_general(a, b, (((1,), (1,)), ((), ())), preferred_element_type=F32)


def _dot_tn(a, b):
    return lax.dot_general(a, b, (((0,), (0,)), ((), ())), preferred_element_type=F32)


def _mod_kernel(c_ref, w_ref, b_ref, o_ref):
    cs = _silu(c_ref[...]).astype(BF16)
    o_ref[...] = _dot(cs, w_ref[...].astype(BF16)) + b_ref[...]


def _modulation(c_all, w_mod, b_mod, chunk):
    depth, d, n = w_mod.shape
    rows = c_all.shape[0]
    tn = 512
    per = chunk // tn
    return pl.pallas_call(
        _mod_kernel,
        grid=(depth, n // tn),
        in_specs=[
            pl.BlockSpec((rows, d), lambda l, j: (0, 0)),
            pl.BlockSpec((None, d, tn), lambda l, j: (l, 0, j)),
            pl.BlockSpec((None, 1, tn), lambda l, j: (l, 0, j)),
        ],
        out_specs=pl.BlockSpec((None, None, rows, tn), lambda l, j: (l, j // per, 0, j % per)),
        out_shape=jax.ShapeDtypeStruct((depth, n // chunk, rows, chunk), F32),
        compiler_params=_params("arbitrary", "arbitrary"),
        name="modulation",
    )(c_all, w_mod, b_mod.reshape(depth, 1, n))


def _proj_kernel(*refs, pro, epi, emit_lhs, n_extra, out_dtype):
    it = iter(refs)
    if pro == "normmod":
        x_ref, g_ref, sc_ref, sh_ref = next(it), next(it), next(it), next(it)
    elif pro == "gated":
        y_ref, z_ref, g_ref = next(it), next(it), next(it)
    else:
        x_ref = next(it)
    w_ref = next(it)
    b_ref = next(it) if epi in ("bias", "resid") else None
    if epi == "resid":
        xres_ref, gate_ref = next(it), next(it)
    wx_ref = next(it) if n_extra else None
    o_ref = next(it)
    lhs_out_ref = next(it) if emit_lhs else None
    ox_ref = next(it) if n_extra else None
    lhs_ref = next(it) if pro != "plain" else None

    j = pl.program_id(1)

    if pro != "plain":
        @pl.when(j == 0)
        def _():
            if pro == "normmod":
                x = x_ref[...]
                h = _rms(x) * g_ref[...] * (1.0 + sc_ref[...]) + sh_ref[...]
            else:
                y = y_ref[...]
                h = _rms(y * _silu(z_ref[...])) * g_ref[...]
            hb = h.astype(BF16)
            lhs_ref[...] = hb
            if emit_lhs:
                lhs_out_ref[...] = hb
            if n_extra:
                ox_ref[...] = _dot(hb, wx_ref[...])
        lhs = lhs_ref[...]
    else:
        lhs = x_ref[...]

    acc = _dot(lhs, w_ref[...])
    if b_ref is not None:
        acc = acc + b_ref[...]
    if epi == "resid":
        acc = xres_ref[...] + gate_ref[...] * acc
    o_ref[...] = acc.astype(out_dtype)


def _projection(pro, epi, ins, w, *, tm, tn, rows_per_mod=None, bias=None, resid=None,
                w_extra=None, emit_lhs=False, out_dtype=F32, name="proj"):
    k, n = w.shape
    m = ins[0].shape[0]
    assert m % tm == 0 and n % tn == 0
    grid = (m // tm, n // tn)

    def mod_spec(width_block, col_of):
        if rows_per_mod == 1:
            return pl.BlockSpec((tm, width_block), lambda i, j: (i, col_of(j)))
        assert rows_per_mod % tm == 0
        t = rows_per_mod // tm
        return pl.BlockSpec((None, 1, width_block), lambda i, j: (i // t, 0, col_of(j)))

    def mod_arr(a):
        return a if rows_per_mod == 1 else a.reshape(a.shape[0], 1, a.shape[1])

    args, specs = [], []
    if pro == "normmod":
        x, g, sc, sh = ins
        args += [x, g.reshape(1, k), mod_arr(sc), mod_arr(sh)]
        specs += [pl.BlockSpec((tm, k), lambda i, j: (i, 0)),
                  pl.BlockSpec((1, k), lambda i, j: (0, 0)),
                  mod_spec(k, lambda j: 0), mod_spec(k, lambda j: 0)]
    elif pro == "gated":
        y, z, g = ins
        args += [y, z, g.reshape(1, k)]
        specs += [pl.BlockSpec((tm, k), lambda i, j: (i, 0)),
                  pl.BlockSpec((tm, k), lambda i, j: (i, 0)),
                  pl.BlockSpec((1, k), lambda i, j: (0, 0))]
    else:
        (x,) = ins
        args += [x]
        specs += [pl.BlockSpec((tm, k), lambda i, j: (i, 0))]
    args.append(w)
    specs.append(pl.BlockSpec((k, tn), lambda i, j: (0, j)))
    if epi in ("bias", "resid"):
        b = bias if bias is not None else jnp.zeros((n,), F32)
        args.append(b.reshape(1, n))
        specs.append(pl.BlockSpec((1, tn), lambda i, j: (0, j)))
    if epi == "resid":
        xres, gate = resid
        args += [xres, mod_arr(gate)]
        specs += [pl.BlockSpec((tm, tn), lambda i, j: (i, j)), mod_spec(tn, lambda j: j)]
    n_extra = 0
    if w_extra is not None:
        n_extra = w_extra.shape[1]
        args.append(w_extra)
        specs.append(pl.BlockSpec((k, n_extra), lambda i, j: (0, 0)))

    out_shapes = [jax.ShapeDtypeStruct((m, n), out_dtype)]
    out_specs = [pl.BlockSpec((tm, tn), lambda i, j: (i, j))]
    if emit_lhs:
        out_shapes.append(jax.ShapeDtypeStruct((m, k), BF16))
        out_specs.append(pl.BlockSpec((tm, k), lambda i, j: (i, 0)))
    if n_extra:
        out_shapes.append(jax.ShapeDtypeStruct((m, n_extra), F32))
        out_specs.append(pl.BlockSpec((tm, n_extra), lambda i, j: (i, 0)))
    scratch = [pltpu.VMEM((tm, k), BF16)] if pro != "plain" else []

    outs = pl.pallas_call(
        functools.partial(_proj_kernel, pro=pro, epi=epi, emit_lhs=emit_lhs, n_extra=n_extra,
                          out_dtype=out_dtype),
        grid=grid, in_specs=specs, out_specs=out_specs, out_shape=out_shapes,
        scratch_shapes=scratch,
        compiler_params=_params("arbitrary", "arbitrary"),
        name=name,
    )(*args)
    return outs if len(outs) > 1 else outs[0]


def _rglru_gates(xc, wa_ref, ba_ref, wi_ref, bi_ref, lam_ref):
    xcb = xc.astype(BF16)
    r = jax.nn.sigmoid(_dot(xcb, wa_ref[...]) + ba_ref[...])
    i = jax.nn.sigmoid(_dot(xcb, wi_ref[...]) + bi_ref[...])
    log_a = (-RG_C * _softplus(-lam_ref[...])) * r
    a = jnp.exp(log_a)
    mult = jnp.sqrt(jnp.maximum(-jnp.tanh(log_a) * (a * a + 1.0), 0.0))
    return a, mult * (i * xc)


RG_ROWS = 256


def _rglru_seq_kernel(gate_ref, xr_ref, cw_ref, cb_ref, wa_ref, ba_ref, wi_ref, bi_ref, lam_ref,
                      y_ref, hlast_ref, xpad_ref):
    seq, blk = xr_ref.shape
    pad = SUBLANES
    xpad_ref[0:pad, :] = jnp.zeros((pad, blk), F32)
    xpad_ref[pad:pad + seq, :] = xr_ref[...]
    row_in_tile = lax.broadcasted_iota(jnp.int32, (RG_ROWS, blk), 0) % SUBLANES

    def step(c, carry):
        start = pl.multiple_of(c * RG_ROWS, RG_ROWS)
        xe = xpad_ref[pl.ds(start, RG_ROWS + pad), :]
        xc = cb_ref[...] + cw_ref[3:4, :] * xe[pad:]
        for k in range(1, CONV_W):
            xc = xc + cw_ref[3 - k:4 - k, :] * pltpu.roll(xe, k, axis=0)[pad:]
        a, b = _rglru_gates(xc, wa_ref, ba_ref, wi_ref, bi_ref, lam_ref)
        for s in (1, 2, 4):
            m = row_in_tile >= s
            b = jnp.where(m, a * pltpu.roll(b, s, axis=0) + b, b)
            a = jnp.where(m, a * pltpu.roll(a, s, axis=0), a)
        hs = []
        for t in range(RG_ROWS // SUBLANES):
            sl = slice(t * SUBLANES, (t + 1) * SUBLANES)
            h = b[sl] + a[sl] * carry
            hs.append(h)
            carry = h[SUBLANES - 1:SUBLANES, :]
        h_all = jnp.concatenate(hs, axis=0)
        y_ref[pl.ds(start, RG_ROWS), :] = (h_all * jax.nn.gelu(gate_ref[pl.ds(start, RG_ROWS), :])).astype(BF16)
        return carry

    carry = lax.fori_loop(0, seq // RG_ROWS, step, jnp.zeros((1, blk), F32))
    hlast_ref[...] = carry


def _rglru_seq(proj, conv_w, conv_b, w_a, b_a, w_i, b_i, lam):
    nb, seq, two_d = proj.shape
    d = two_d // 2
    blk = d // RG_BLOCKS
    vec = lambda a: a.reshape(1, d)
    vspec = pl.BlockSpec((1, blk), lambda b, k: (0, k))
    wspec = pl.BlockSpec((None, blk, blk), lambda b, k: (k, 0, 0))
    return pl.pallas_call(
        _rglru_seq_kernel,
        grid=(nb, RG_BLOCKS),
        in_specs=[
            pl.BlockSpec((None, seq, blk), lambda b, k: (b, 0, k)),
            pl.BlockSpec((None, seq, blk), lambda b, k: (b, 0, RG_BLOCKS + k)),
            pl.BlockSpec((CONV_W, blk), lambda b, k: (0, k)),
            vspec, wspec, vspec, wspec, vspec, vspec,
        ],
        out_specs=[
            pl.BlockSpec((None, seq, blk), lambda b, k: (b, 0, k)),
            pl.BlockSpec((None, 1, blk), lambda b, k: (b, 0, k)),
        ],
        out_shape=[jax.ShapeDtypeStruct((nb, seq, d), BF16), jax.ShapeDtypeStruct((nb, 1, d), F32)],
        scratch_shapes=[pltpu.VMEM((seq + SUBLANES, blk), F32)],
        compiler_params=_params("arbitrary", "arbitrary"),
        name="rglru_seq",
    )(proj, proj, conv_w, vec(conv_b), w_a.astype(BF16), vec(b_a), w_i.astype(BF16), vec(b_i), vec(lam))


def _rglru_step_kernel(gate_ref, xr_ref, c0_ref, c1_ref, c2_ref, h0_ref, cw_ref, cb_ref,
                       wa_ref, ba_ref, wi_ref, bi_ref, lam_ref, y_ref, h_ref):
    xc = (cb_ref[...] + cw_ref[0:1, :] * c0_ref[...] + cw_ref[1:2, :] * c1_ref[...]
          + cw_ref[2:3, :] * c2_ref[...] + cw_ref[3:4, :] * xr_ref[...])
    a, b = _rglru_gates(xc, wa_ref, ba_ref, wi_ref, bi_ref, lam_ref)
    h = a * h0_ref[...] + b
    h_ref[...] = h
    y_ref[...] = (h * jax.nn.gelu(gate_ref[...])).astype(BF16)


def _rglru_step(proj, conv_state, h0, conv_w, conv_b, w_a, b_a, w_i, b_i, lam):
    nb, two_d = proj.shape
    d = two_d // 2
    blk = d // RG_BLOCKS
    vec = lambda a: a.reshape(1, d)
    rows = pl.BlockSpec((nb, blk), lambda k: (0, k))
    vspec = pl.BlockSpec((1, blk), lambda k: (0, k))
    wspec = pl.BlockSpec((None, blk, blk), lambda k: (k, 0, 0))
    return pl.pallas_call(
        _rglru_step_kernel,
        grid=(RG_BLOCKS,),
        in_specs=[rows, pl.BlockSpec((nb, blk), lambda k: (0, RG_BLOCKS + k)), rows, rows, rows, rows,
                  pl.BlockSpec((CONV_W, blk), lambda k: (0, k)),
                  vspec, wspec, vspec, wspec, vspec, vspec],
        out_specs=[rows, rows],
        out_shape=[jax.ShapeDtypeStruct((nb, d), BF16), jax.ShapeDtypeStruct((nb, d), F32)],
        compiler_params=_params("arbitrary"),
        name="rglru_step",
    )(proj, proj, conv_state[:, 0], conv_state[:, 1], conv_state[:, 2], h0, conv_w, vec(conv_b),
      w_a.astype(BF16), vec(b_a), w_i.astype(BF16), vec(b_i), vec(lam))


def _split3(x):
    hi = x.astype(BF16)
    r1 = x - hi.astype(F32)
    mid = r1.astype(BF16)
    lo = (r1 - mid.astype(F32)).astype(BF16)
    return hi, mid, lo


def _head_expand(x, expand_ref):
    hi = x.astype(BF16)
    lo = (x - hi.astype(F32)).astype(BF16)
    return _dot(hi, expand_ref[...]) + _dot(lo, expand_ref[...])


def _ssd_seq_kernel(xbc_ref, halo_ref, dt_ref, z_ref, cw_ref, cb_ref, dtb_ref, alog_ref, drep_ref, g_ref,
                    expand_ref, yn_ref, st_ref, *, d_inner):
    c = pl.program_id(1)
    q = SSD_CHUNK
    p = SSD_HEAD_DIM
    n = SSD_STATE
    hpg = d_inner // p // SSD_GROUPS
    epp = hpg * p
    pad = SUBLANES

    @pl.when(c == 0)
    def _():
        st_ref[...] = jnp.zeros(st_ref.shape, F32)

    halo = jnp.where(c > 0, halo_ref[...], 0.0)
    xe = jnp.concatenate([halo, xbc_ref[...]], axis=0)
    xc = cb_ref[...] + cw_ref[3:4, :] * xe[pad:]
    for k in range(1, CONV_W):
        xc = xc + cw_ref[3 - k:4 - k, :] * pltpu.roll(xe, k, axis=0)[pad:]
    xc = _silu(xc)
    xs = xc[:, :d_inner]

    dt = _softplus(dt_ref[...] + dtb_ref[...])
    da = dt * (-jnp.exp(alog_ref[...]))
    ri = lax.broadcasted_iota(jnp.int32, (q, q), 0)
    ci = lax.broadcasted_iota(jnp.int32, (q, q), 1)
    tri = ri >= ci
    ltri = jnp.where(tri, 1.0, 0.0).astype(BF16)
    hi, mid, lo = _split3(da)
    cum = _dot(ltri, hi) + _dot(ltri, mid) + _dot(ltri, lo)
    cum_t = cum.T
    dt_t = dt.T
    last = cum[q - 1:q, :]
    ecum_x = _head_expand(jnp.exp(cum), expand_ref)
    wend_x = _head_expand(jnp.exp(last - cum) * dt, expand_ref)
    elast_x = _head_expand(jnp.broadcast_to(jnp.exp(last), (SUBLANES, last.shape[1])), expand_ref)[0:1, :]
    xw = (xs * wend_x).astype(BF16)
    first_head = lax.broadcasted_iota(jnp.int32, (q, 2 * p), 1) < p

    y_groups = []
    for g in range(SSD_GROUPS):
        bb = xc[:, d_inner + g * n:d_inner + (g + 1) * n].astype(BF16)
        cc = xc[:, d_inner + SSD_GROUPS * n + g * n:d_inner + SSD_GROUPS * n + (g + 1) * n].astype(BF16)
        cb = _dot_nt(cc, bb)
        st = st_ref[g]
        ys = _dot(cc, st.astype(BF16))
        pairs = []
        for e in range(0, hpg, 2):
            xpair = xs[:, (g * hpg + e) * p:(g * hpg + e + 2) * p].astype(BF16)
            acc = None
            for k in range(2):
                h = g * hpg + e + k
                decay = jnp.exp(jnp.where(tri, cum[:, h:h + 1] - cum_t[h:h + 1, :], NEG_INF))
                w = (decay * cb * dt_t[h:h + 1, :]).astype(BF16)
                keep = first_head if k == 0 else jnp.logical_not(first_head)
                term = _dot(w, jnp.where(keep, xpair, jnp.zeros((), BF16)))
                acc = term if acc is None else acc + term
            pairs.append(acc)
        lanes = slice(g * epp, (g + 1) * epp)
        y_groups.append(jnp.concatenate(pairs, axis=1) + ys * ecum_x[:, lanes])
        st_ref[g] = st * elast_x[:, lanes] + _dot_tn(bb, xw[:, lanes])
    y = jnp.concatenate(y_groups, axis=1) + drep_ref[...] * xs
    yn_ref[...] = (_rms(y * _silu(z_ref[...])) * g_ref[...]).astype(BF16)


def _ssd_seq(xbc, dt_raw, z, conv_w, conv_b, dt_bias, a_log, d_skip, norm_g, d_inner):
    nb, seq, cdim = xbc.shape
    q = SSD_CHUNK
    hp = dt_raw.shape[-1]
    heads = d_inner // SSD_HEAD_DIM
    epp = d_inner // SSD_GROUPS
    padv = lambda a: jnp.pad(a.reshape(1, heads), ((0, 0), (0, hp - heads)))
    halo_blocks = q // SUBLANES
    expand = (jnp.arange(d_inner)[None, :] // SSD_HEAD_DIM == jnp.arange(hp)[:, None]).astype(BF16)
    const = lambda shape: pl.BlockSpec(shape, lambda b, c: (0, 0))
    return pl.pallas_call(
        functools.partial(_ssd_seq_kernel, d_inner=d_inner),
        grid=(nb, seq // q),
        in_specs=[
            pl.BlockSpec((None, q, cdim), lambda b, c: (b, c, 0)),
            pl.BlockSpec((None, SUBLANES, cdim), lambda b, c: (b, jnp.maximum(c * halo_blocks - 1, 0), 0)),
            pl.BlockSpec((None, q, hp), lambda b, c: (b, c, 0)),
            pl.BlockSpec((None, q, d_inner), lambda b, c: (b, c, 0)),
            const((CONV_W, cdim)), const((1, cdim)), const((1, hp)), const((1, hp)),
            const((1, d_inner)), const((1, d_inner)), const((hp, d_inner)),
        ],
        out_specs=[
            pl.BlockSpec((None, q, d_inner), lambda b, c: (b, c, 0)),
            pl.BlockSpec((None, SSD_GROUPS, SSD_STATE, epp), lambda b, c: (b, 0, 0, 0)),
        ],
        out_shape=[jax.ShapeDtypeStruct((nb, seq, d_inner), BF16),
                   jax.ShapeDtypeStruct((nb, SSD_GROUPS, SSD_STATE, epp), F32)],
        compiler_params=_params("arbitrary", "arbitrary"),
        name="ssd_seq",
    )(xbc, xbc, dt_raw, z, conv_w, conv_b.reshape(1, cdim), padv(dt_bias), padv(a_log),
      jnp.repeat(d_skip, SSD_HEAD_DIM).reshape(1, d_inner), norm_g.reshape(1, d_inner), expand)


def _ssd_step_pre_kernel(xbc_ref, c0_ref, c1_ref, c2_ref, dt_ref, cw_ref, cb_ref, dtb_ref, alog_ref, drep_ref,
                         xdt_ref, skip_ref, bc_ref, dec_ref, *, d_inner):
    xc = (cb_ref[...] + cw_ref[0:1, :] * c0_ref[...] + cw_ref[1:2, :] * c1_ref[...]
          + cw_ref[2:3, :] * c2_ref[...] + cw_ref[3:4, :] * xbc_ref[...])
    xc = _silu(xc)
    dt = _softplus(dt_ref[...] + dtb_ref[...])
    dec_ref[...] = jnp.exp(dt * (-jnp.exp(alog_ref[...])))
    hp = dt.shape[1]
    head_of_lane = lax.broadcasted_iota(jnp.int32, (hp, d_inner), 1) // SSD_HEAD_DIM
    expand = jnp.where(head_of_lane == lax.broadcasted_iota(jnp.int32, (hp, d_inner), 0), 1.0, 0.0).astype(BF16)
    dt_rep = sum(_dot(piece, expand) for piece in _split3(dt))
    xs = xc[:, :d_inner]
    xdt_ref[...] = xs * dt_rep
    skip_ref[...] = xs * drep_ref[...]
    bc_ref[...] = xc[:, d_inner:]


def _ssd_step_state_kernel(dec_ref, s_ref, xdt_ref, skip_ref, b_ref, c_ref, s_out_ref, y_ref):
    tb, npair, rows, n = s_ref.shape
    pairs_per_group = npair // SSD_GROUPS
    seq0 = pl.program_id(0) * tb
    row_id = lax.broadcasted_iota(jnp.int32, (rows, n), 0)
    pair_id = lax.broadcasted_iota(jnp.int32, (npair, n), 0)
    for b in range(tb):
        xt = xdt_ref[b].T.astype(BF16)
        bm = b_ref[b]
        brep = jnp.concatenate([jnp.broadcast_to(bm[g:g + 1], (pairs_per_group, n))
                                for g in range(SSD_GROUPS)], axis=0)
        bbig = jnp.concatenate([jnp.where(pair_id == j, brep, 0.0) for j in range(npair)], axis=1)
        upd = _dot(xt, bbig.astype(BF16))
        cm = c_ref[b].astype(BF16)
        for j in range(npair):
            g = j // pairs_per_group
            d0 = dec_ref[seq0 + b, 2 * j]
            d1 = dec_ref[seq0 + b, 2 * j + 1]
            s_new = s_ref[b, j] * jnp.where(row_id < rows // 2, d0, d1) + upd[:, j * n:(j + 1) * n]
            s_out_ref[b, j] = s_new
            y8 = _dot_nt(cm, s_new.astype(BF16))
            y_ref[b, j:j + 1, :] = y8[g:g + 1, :] + skip_ref[b, j:j + 1, :]


def _ssd_step(xbc_raw, conv_state, dt_raw, state, conv_w, conv_b, dt_bias, a_log, d_skip, d_inner):
    nb, cdim = xbc_raw.shape
    hp = dt_raw.shape[-1]
    heads = d_inner // SSD_HEAD_DIM
    p, n = SSD_HEAD_DIM, SSD_STATE
    gn = SSD_GROUPS * n
    padv = lambda a: jnp.pad(a.reshape(1, heads), ((0, 0), (0, hp - heads)))
    full = lambda shape: pl.BlockSpec(shape, lambda: tuple(0 for _ in shape))
    xdt, skip, bc, dec = pl.pallas_call(
        functools.partial(_ssd_step_pre_kernel, d_inner=d_inner),
        in_specs=[full((nb, cdim))] * 4 + [full((nb, hp)), full((CONV_W, cdim)), full((1, cdim)),
                                            full((1, hp)), full((1, hp)), full((1, d_inner))],
        out_specs=[full((nb, d_inner)), full((nb, d_inner)), full((nb, 2 * gn)), full((nb, hp))],
        out_shape=[jax.ShapeDtypeStruct((nb, d_inner), F32), jax.ShapeDtypeStruct((nb, d_inner), F32),
                   jax.ShapeDtypeStruct((nb, 2 * gn), F32), jax.ShapeDtypeStruct((nb, hp), F32)],
        compiler_params=pltpu.CompilerParams(vmem_limit_bytes=VMEM_LIMIT),
        name="ssd_step_pre",
    )(xbc_raw, conv_state[:, 0], conv_state[:, 1], conv_state[:, 2], dt_raw, conv_w,
      conv_b.reshape(1, cdim), padv(dt_bias), padv(a_log), jnp.repeat(d_skip, p).reshape(1, d_inner))
    npair = heads // 2
    pair3 = lambda a: a.reshape(nb, npair, 2 * p)
    tb = 2
    sspec = pl.BlockSpec((tb, npair, 2 * p, n), lambda i: (i, 0, 0, 0))
    vspec = pl.BlockSpec((tb, npair, 2 * p), lambda i: (i, 0, 0))
    gspec = pl.BlockSpec((tb, SSD_GROUPS, n), lambda i: (i, 0, 0))
    s_new, y3 = pl.pallas_call(
        _ssd_step_state_kernel,
        grid=(nb // tb,),
        in_specs=[pl.BlockSpec(memory_space=pltpu.SMEM), sspec, vspec, vspec, gspec, gspec],
        out_specs=[sspec, vspec],
        out_shape=[jax.ShapeDtypeStruct((nb, npair, 2 * p, n), F32),
                   jax.ShapeDtypeStruct((nb, npair, 2 * p), F32)],
        compiler_params=_params("arbitrary"),
        name="ssd_step_state",
    )(dec, state.reshape(nb, npair, 2 * p, n), pair3(xdt), pair3(skip),
      bc[:, :gn].reshape(nb, SSD_GROUPS, n), bc[:, gn:].reshape(nb, SSD_GROUPS, n))
    return y3.reshape(nb, d_inner), s_new


NOT_RANKED = 99.0


def _tree(op, xs):
    xs = list(xs)
    while len(xs) > 1:
        xs = [op(xs[i], xs[i + 1]) if i + 1 < len(xs) else xs[i] for i in range(0, len(xs), 2)]
    return xs[0]
PEER_PAIRS = [(a, b) for a in range(PEER_TOPK) for b in range(PEER_TOPK) if (a + 1) * (b + 1) <= PEER_TOPK]


def _sort_network(n):
    pairs = []
    p = 1
    while p < n:
        k = p
        while k >= 1:
            for j in range(k % p, n - k, 2 * k):
                for i in range(min(k, n - j - k)):
                    if (i + j) // (2 * p) == (i + j + k) // (2 * p):
                        pairs.append((i + j, i + j + k))
            k //= 2
        p *= 2
    return pairs


SORT16 = _sort_network(PEER_TOPK)
BITONIC16 = [(i, i + st) for st in (8, 4, 2, 1) for i in range(PEER_TOPK) if (i // st) % 2 == 0]


def _exchange(xs, pairs):
    for i, j in pairs:
        xs[i], xs[j] = jnp.maximum(xs[i], xs[j]), jnp.minimum(xs[i], xs[j])
    return xs


def _top16_values(w):
    groups = [_exchange(list(w[g:g + PEER_TOPK]), SORT16) for g in range(0, len(w), PEER_TOPK)]
    dropped = []
    while len(groups) > 1:
        merged = []
        for a, b in zip(groups[0::2], groups[1::2]):
            hi = [jnp.maximum(a[i], b[PEER_TOPK - 1 - i]) for i in range(PEER_TOPK)]
            dropped.append(_tree(jnp.maximum, [jnp.minimum(a[i], b[PEER_TOPK - 1 - i]) for i in range(PEER_TOPK)]))
            merged.append(_exchange(hi, BITONIC16))
        groups = merged
    return groups[0], _tree(jnp.maximum, dropped)


def _staircase(v1, v2, shape):
    cand = {c: v1[c[0]] + v2[c[1]] for c in PEER_PAIRS}
    ahead_of = {c: [] for c in PEER_PAIRS}
    behind_of = {c: [] for c in PEER_PAIRS}
    for n1, c in enumerate(PEER_PAIRS):
        for c2 in PEER_PAIRS[n1 + 1:]:
            if c2[0] >= c[0] and c2[1] >= c[1]:
                continue
            ahead = jnp.where(cand[c] >= cand[c2], 1.0, 0.0)
            ahead_of[c2].append(ahead)
            behind_of[c].append(ahead)
    e1v = [jnp.exp(v1[a] - v1[0]) for a in range(PEER_TOPK)]
    e2v = [jnp.exp(v2[b] - v2[0]) for b in range(PEER_TOPK)]
    keep_terms = [[] for _ in range(PEER_TOPK)]
    z_terms = []
    for c in PEER_PAIRS:
        cnt = jnp.full(shape, float((c[0] + 1) * (c[1] + 1) - 1 + len(behind_of[c])), F32)
        if ahead_of[c]:
            cnt = cnt + _tree(jnp.add, ahead_of[c])
        if behind_of[c]:
            cnt = cnt - _tree(jnp.add, behind_of[c])
        keep = jnp.where(cnt < float(PEER_TOPK), 1.0, 0.0)
        keep_terms[c[0]].append(keep)
        z_terms.append(keep * (e1v[c[0]] * e2v[c[1]]))
    return [_tree(jnp.add, t) for t in keep_terms], 1.0 / _tree(jnp.add, z_terms)


def _peer_prep_kernel(q_ref, k1_ref, k2_ref, r2_ref, e2_ref, c1_ref, e1_ref,
                      s_scr, work_scr, rank_scr, v_scr):
    tt = q_ref.shape[0]
    hh = PEER_HEADS
    half = k1_ref.shape[1]
    tile = lambda k: slice(k * hh, (k + 1) * hh)
    twice = lambda a: jnp.concatenate([a, a], axis=0)

    for side, k_ref in enumerate((k1_ref, k2_ref)):
        s_scr[side] = _dot_nt(k_ref[...], q_ref[:, side * half:(side + 1) * half])

    tops, ties = [], []
    for side in range(2):
        v, nxt = _top16_values([s_scr[side, tile(k), :] for k in range(PEER_NKEYS)])
        tops.append(v)
        ties += [jnp.where(v[a] == v[a + 1], 1.0, 0.0) for a in range(PEER_TOPK - 1)]
        ties.append(jnp.where(v[PEER_TOPK - 1] == nxt, 1.0, 0.0))
    v1, v2 = tops
    ncols, rz = _staircase(v1, v2, (hh, tt))
    inf = jnp.full((hh, tt), float("inf"), F32)
    thr = [_tree(jnp.minimum, [jnp.where(ncols[a] > float(m), v1[a], inf)
                               for a in range(PEER_TOPK) if (a + 1) * (m + 1) <= PEER_TOPK])
           for m in range(PEER_TOPK)]
    thr2 = [twice(t) for t in thr]
    v2x = [twice(v) for v in v2]
    m1, m2, rz2 = twice(v1[0]), twice(v2[0]), twice(rz)
    for k in range(0, PEER_NKEYS, 2):
        rows = slice(k * hh, (k + 2) * hh)
        s1 = s_scr[0, rows, :]
        s2 = s_scr[1, rows, :]
        c1_ref[rows, :] = _tree(jnp.add, [jnp.where(s1 >= t, 1.0, 0.0) for t in thr2])
        r2_ref[rows, :] = _tree(jnp.add, [jnp.where(v > s2, 1.0, 0.0) for v in v2x]).astype(BF16)
        e1_ref[rows, :] = jnp.exp(s1 - m1) * rz2
        e2_ref[rows, :] = jnp.exp(s2 - m2).astype(BF16)

    @pl.when(jnp.max(_tree(jnp.maximum, ties)) > 0.0)
    def _():
        for side in range(2):
            work_scr[side] = s_scr[side]
            rank_scr[side] = jnp.full(work_scr.shape[1:], NOT_RANKED, F32)

        def extract(a, _):
            af = jnp.asarray(a, F32)
            for side in range(2):
                w = [work_scr[side, tile(k), :] for k in range(PEER_NKEYS)]
                m = _tree(jnp.maximum, w)
                sel = _tree(jnp.minimum, [jnp.where(w[k] == m, k, PEER_NKEYS) for k in range(PEER_NKEYS)])
                for k in range(PEER_NKEYS):
                    hit = sel == k
                    work_scr[side, tile(k), :] = jnp.where(hit, NEG_INF, w[k])
                    rank_scr[side, tile(k), :] = jnp.where(hit, af, rank_scr[side, tile(k), :])
                v_scr[side, a] = m
            return 0

        lax.fori_loop(0, PEER_TOPK, extract, 0)
        u1 = [v_scr[0, a] for a in range(PEER_TOPK)]
        u2 = [v_scr[1, b] for b in range(PEER_TOPK)]
        ncols_t, rz_t = _staircase(u1, u2, (hh, tt))
        ncols2 = [twice(c) for c in ncols_t]
        rzt2 = twice(rz_t)
        for k in range(0, PEER_NKEYS, 2):
            rows = slice(k * hh, (k + 2) * hh)
            r1 = rank_scr[0, rows, :]
            c1 = jnp.zeros((2 * hh, tt), F32)
            for a in range(PEER_TOPK):
                c1 = jnp.where(r1 == float(a), ncols2[a], c1)
            c1_ref[rows, :] = c1
            e1_ref[rows, :] = jnp.exp(s_scr[0, rows, :] - m1) * rzt2
            r2_ref[rows, :] = rank_scr[1, rows, :].astype(BF16)


def _peer_prep(qp, k1big, k2big, tt=LANES):
    t = qp.shape[0]
    rows = k1big.shape[0]
    out = lambda dt: jax.ShapeDtypeStruct((rows, t), dt)
    ospec = pl.BlockSpec((rows, tt), lambda i: (0, i))
    return pl.pallas_call(
        _peer_prep_kernel,
        grid=(t // tt,),
        in_specs=[pl.BlockSpec((tt, qp.shape[1]), lambda i: (i, 0)),
                  pl.BlockSpec(k1big.shape, lambda i: (0, 0)),
                  pl.BlockSpec(k2big.shape, lambda i: (0, 0))],
        out_specs=[ospec] * 4,
        out_shape=[out(BF16), out(BF16), out(F32), out(F32)],
        scratch_shapes=[pltpu.VMEM((2, rows, tt), F32), pltpu.VMEM((2, rows, tt), F32),
                        pltpu.VMEM((2, rows, tt), F32), pltpu.VMEM((2, PEER_TOPK, PEER_HEADS, tt), F32)],
        compiler_params=_params("arbitrary"),
        name="peer_prep",
    )(qp, k1big, k2big)


def _peer_expert_kernel(hc_ref, u_ref, vt_ref, r2_ref, e2_ref, c1_ref, e1_ref, x_ref, gate_ref, *rest):
    *final_g_ref, o_ref, acc_scr = rest
    e = pl.program_id(1)
    te = u_ref.shape[0]
    nk = PEER_NKEYS

    @pl.when(e == 0)
    def _():
        acc_scr[...] = jnp.zeros(acc_scr.shape, F32)

    act = jax.nn.gelu(_dot_nt(u_ref[...], hc_ref[...])).astype(BF16)
    blocks = []
    for ii in range(te // nk):
        w = None
        for h in range(PEER_HEADS):
            row = ii * PEER_HEADS + h
            c1 = c1_ref[row:row + 1, :].astype(BF16)
            e1 = e1_ref[row:row + 1, :].astype(BF16)
            term = jnp.where(r2_ref[h] < c1, e2_ref[h], jnp.zeros((), BF16)) * e1
            w = term if w is None else w + term
        blocks.append(act[ii * nk:(ii + 1) * nk] * w)
    acc_scr[...] += _dot(vt_ref[...], jnp.concatenate(blocks, axis=0))

    @pl.when(e == pl.num_programs(1) - 1)
    def _():
        y = x_ref[...] + gate_ref[...] * acc_scr[...].T
        o_ref[...] = _rms(y) * final_g_ref[0][...] if final_g_ref else y


def _peer_expert(hc, u_bf, vt_bf, layer, r2, e2, c1, e1, x, gate, *, tm, te, rows_per_mod, final_g=None):
    t, d = hc.shape
    n_exp = u_bf.shape[1]
    rows_i = te // PEER_NKEYS * PEER_HEADS
    hh, nk = PEER_HEADS, PEER_NKEYS
    once = pl.Buffered(1)
    if rows_per_mod == 1:
        gate_arr = gate
        gate_spec = pl.BlockSpec((tm, d), lambda i, e: (i, 0), pipeline_mode=once)
    else:
        per = rows_per_mod // tm
        gate_arr = gate.reshape(gate.shape[0], 1, d)
        gate_spec = pl.BlockSpec((None, 1, d), lambda i, e: (i // per, 0, 0))
    return pl.pallas_call(
        _peer_expert_kernel,
        grid=(t // tm, n_exp // te),
        in_specs=[
            pl.BlockSpec((tm, d), lambda i, e: (i, 0), pipeline_mode=once),
            pl.BlockSpec((None, te, d), lambda i, e: (layer, e, 0)),
            pl.BlockSpec((None, d, te), lambda i, e: (layer, 0, e)),
            pl.BlockSpec((hh, nk, tm), lambda i, e: (0, 0, i), pipeline_mode=once),
            pl.BlockSpec((hh, nk, tm), lambda i, e: (0, 0, i), pipeline_mode=once),
            pl.BlockSpec((rows_i, tm), lambda i, e: (e, i)),
            pl.BlockSpec((rows_i, tm), lambda i, e: (e, i)),
            pl.BlockSpec((tm, d), lambda i, e: (i, 0), pipeline_mode=once),
            gate_spec,
        ] + ([pl.BlockSpec((1, d), lambda i, e: (0, 0))] if final_g is not None else []),
        out_specs=pl.BlockSpec((tm, d), lambda i, e: (i, 0)),
        out_shape=jax.ShapeDtypeStruct((t, d), F32),
        scratch_shapes=[pltpu.VMEM((d, tm), F32)],
        compiler_params=_params("arbitrary", "arbitrary"),
        name="peer_expert",
    )(hc, u_bf, vt_bf, r2, e2, c1, e1, x, gate_arr, *([final_g.reshape(1, d)] if final_g is not None else []))


def _transpose_cast_kernel(x_ref, o_ref):
    o_ref[...] = x_ref[...].T.astype(BF16)


def _transpose_cast(x, tr):
    depth, rows, cols = x.shape
    return pl.pallas_call(
        _transpose_cast_kernel,
        grid=(depth, rows // tr),
        in_specs=[pl.BlockSpec((None, tr, cols), lambda l, i: (l, i, 0))],
        out_specs=pl.BlockSpec((None, cols, tr), lambda l, i: (l, 0, i)),
        out_shape=jax.ShapeDtypeStruct((depth, cols, rows), BF16),
        compiler_params=_params("arbitrary", "arbitrary"),
        name="transpose_cast",
    )(x)


TM_PROMPT = 1024
TN_PROJ = 1024
TM_EXPERT = 512
TE_EXPERT = 1024
TE_EXPERT_SAMPLE = 2048


def _peer_layer(groups, w_q, k1, k2, u_bf, vt_bf, layer, norm_g, final_g=None):
    d = w_q.shape[0]
    hh, nk = PEER_HEADS, PEER_NKEYS
    half = k1.shape[1]
    wq = w_q.reshape(d, hh, 2, half).transpose(0, 2, 1, 3).reshape(d, 2 * hh * half).astype(BF16)
    eye = jnp.eye(hh, dtype=F32)
    kbig = lambda k: jnp.einsum("kd,hg->khgd", k, eye).reshape(nk * hh, hh * half).astype(BF16)
    k1b, k2b = kbig(k1), kbig(k2)
    outs = []
    for gr in groups:
        x = gr["x"]
        t = x.shape[0]
        qp, hc = _projection("normmod", "none", (x, norm_g, gr["sc"], gr["sh"]), wq,
                             tm=gr["tm"], tn=TN_PROJ, rows_per_mod=gr["rows_per_mod"],
                             emit_lhs=True, out_dtype=BF16, name="peer_q")
        r2, e2, c1, e1 = _peer_prep(qp, k1b, k2b)
        r2, e2 = [a.reshape(nk, hh, t).transpose(1, 0, 2) for a in (r2, e2)]
        outs.append(_peer_expert(hc, u_bf, vt_bf, layer, r2, e2, c1, e1, x, gr["gate"],
                                 tm=gr["tme"], te=gr["te"], rows_per_mod=gr["rows_per_mod"], final_g=final_g))
    return outs


def kernel(x_prompt, x_sample, state_rg_conv, state_rg_h, state_ssd_conv, state_ssd, c_prompt, c_sample, norm1_g, norm2_g, w_mod, b_mod, rg_w_in, rg_b_in, rg_conv_w, rg_conv_b, rg_w_a, rg_b_a, rg_w_i, rg_b_i, rg_lambda, rg_w_out, rg_b_out, ssd_w_in, ssd_conv_w, ssd_conv_b, ssd_dt_bias, ssd_a_log, ssd_d, ssd_norm_g, ssd_w_out, peer_w_q, peer_k1, peer_k2, peer_u, peer_v, final_g):
    bp, seq, d = x_prompt.shape
    bs = x_sample.shape[0]
    tp = bp * seq
    depth = w_mod.shape[0]
    d_rnn = rg_w_out.shape[1]
    d_inner = ssd_w_out.shape[1]
    cdim = ssd_conv_w.shape[-1]
    heads = d_inner // SSD_HEAD_DIM

    nc = bp + bs
    c_all = jnp.concatenate([c_sample, c_prompt, jnp.zeros((-nc % SUBLANES, d), F32)], axis=0)
    mod = _modulation(c_all, w_mod, b_mod, d)
    u_bf = peer_u.astype(BF16)
    vt_bf = _transpose_cast(peer_v, 512)

    xp = x_prompt.reshape(tp, d)
    xs = x_sample.reshape(bs, d)
    tms = bs
    outs = {}

    for l in range(depth):
        mp = [mod[l, i, bs:nc] for i in range(6)]
        ms = [mod[l, i, :bs] for i in range(6)]
        j = l // 2
        if l % 2 == 0:
            w_in = rg_w_in[j].astype(BF16)
            w_out = rg_w_out[j].astype(BF16)
            rg = (rg_conv_w[j], rg_conv_b[j], rg_w_a[j], rg_b_a[j], rg_w_i[j], rg_b_i[j], rg_lambda[j])
            proj_p = _projection("normmod", "bias", (xp, norm1_g[l], mp[1], mp[0]), w_in, tm=TM_PROMPT,
                                 tn=TN_PROJ, rows_per_mod=seq, bias=rg_b_in[j], name="rg_in")
            proj_s = _projection("normmod", "bias", (xs, norm1_g[l], ms[1], ms[0]), w_in, tm=tms,
                                 tn=TN_PROJ, rows_per_mod=1, bias=rg_b_in[j], name="rg_in")
            proj_p3 = proj_p.reshape(bp, seq, 2 * d_rnn)
            y_p, h_p = _rglru_seq(proj_p3, *rg)
            y_s, h_s = _rglru_step(proj_s, state_rg_conv[j], state_rg_h[j], *rg)
            xp = _projection("plain", "resid", (y_p.reshape(tp, d_rnn),), w_out, tm=TM_PROMPT, tn=TN_PROJ,
                             rows_per_mod=seq, bias=rg_b_out[j], resid=(xp, mp[2]), name="rg_out")
            xs = _projection("plain", "resid", (y_s,), w_out, tm=tms, tn=TN_PROJ,
                             rows_per_mod=1, bias=rg_b_out[j], resid=(xs, ms[2]), name="rg_out")
            outs.setdefault("rg_conv_p", []).append(proj_p3[:, seq - (CONV_W - 1):, d_rnn:])
            outs.setdefault("rg_h_p", []).append(h_p.reshape(bp, d_rnn))
            outs.setdefault("rg_conv_s", []).append(
                jnp.concatenate([state_rg_conv[j][:, 1:], proj_s[:, None, d_rnn:]], axis=1))
            outs.setdefault("rg_h_s", []).append(h_s)
        else:
            w_in = ssd_w_in[j]
            w_z = w_in[:, :d_inner].astype(BF16)
            w_xbc = w_in[:, d_inner:d_inner + cdim].astype(BF16)
            w_dt = jnp.pad(w_in[:, d_inner + cdim:], ((0, 0), (0, LANES - heads))).astype(BF16)
            w_out = ssd_w_out[j].astype(BF16)
            sp = (ssd_conv_w[j], ssd_conv_b[j], ssd_dt_bias[j], ssd_a_log[j], ssd_d[j])
            z_p, hm_p, dt_p = _projection("normmod", "none", (xp, norm1_g[l], mp[1], mp[0]), w_z, tm=TM_PROMPT,
                                          tn=TN_PROJ, rows_per_mod=seq, w_extra=w_dt, emit_lhs=True, name="ssd_in_z")
            z_s, hm_s, dt_s = _projection("normmod", "none", (xs, norm1_g[l], ms[1], ms[0]), w_z, tm=tms,
                                          tn=TN_PROJ, rows_per_mod=1, w_extra=w_dt, emit_lhs=True, name="ssd_in_z")
            xbc_p = _projection("plain", "none", (hm_p,), w_xbc, tm=TM_PROMPT, tn=TN_PROJ, name="ssd_in_xbc")
            xbc_s = _projection("plain", "none", (hm_s,), w_xbc, tm=tms, tn=TN_PROJ, name="ssd_in_xbc")
            xbc_p3 = xbc_p.reshape(bp, seq, cdim)
            yn_p, st_p = _ssd_seq(xbc_p3, dt_p.reshape(bp, seq, LANES), z_p.reshape(bp, seq, d_inner), *sp,
                                  ssd_norm_g[j], d_inner)
            y_s, st_s = _ssd_step(xbc_s, state_ssd_conv[j], dt_s, state_ssd[j], *sp, d_inner)
            xp = _projection("plain", "resid", (yn_p.reshape(tp, d_inner),), w_out, tm=TM_PROMPT, tn=TN_PROJ,
                             rows_per_mod=seq, resid=(xp, mp[2]), name="ssd_out")
            xs = _projection("gated", "resid", (y_s, z_s, ssd_norm_g[j]), w_out, tm=tms, tn=512,
                             rows_per_mod=1, resid=(xs, ms[2]), name="ssd_out")
            g_, n_ = SSD_GROUPS, SSD_STATE
            outs.setdefault("ssd_conv_p", []).append(xbc_p3[:, seq - (CONV_W - 1):])
            outs.setdefault("ssd_p", []).append(
                st_p.reshape(bp, g_, n_, heads // g_, SSD_HEAD_DIM).transpose(0, 1, 3, 4, 2))
            outs.setdefault("ssd_conv_s", []).append(
                jnp.concatenate([state_ssd_conv[j][:, 1:], xbc_s[:, None]], axis=1))
            outs.setdefault("ssd_s", []).append(st_s.reshape(state_ssd[j].shape))

        xp, xs = _peer_layer(
            [dict(x=xp, sc=mp[4], sh=mp[3], gate=mp[5], rows_per_mod=seq, tm=TM_PROMPT, tme=TM_EXPERT, te=TE_EXPERT),
             dict(x=xs, sc=ms[4], sh=ms[3], gate=ms[5], rows_per_mod=1, tm=tms, tme=tms, te=TE_EXPERT_SAMPLE)],
            peer_w_q[l], peer_k1[l], peer_k2[l], u_bf, vt_bf, l, norm2_g[l],
            final_g=final_g if l == depth - 1 else None)

    y_p = xp.reshape(bp, seq, d)
    y_s = xs.reshape(bs, 1, d)
    st = lambda name: outs[name][0][None] if len(outs[name]) == 1 else jnp.stack(outs[name])
    return (y_p, y_s, st("rg_conv_p"), st("rg_h_p"), st("ssd_conv_p"), st("ssd_p"),
            st("rg_conv_s"), st("rg_h_s"), st("ssd_conv_s"), st("ssd_s"))
```

```python
import functools
import math

import jax
import jax.numpy as jnp
from jax import lax
from jax.experimental import pallas as pl
from jax.experimental.pallas import tpu as pltpu

F32 = jnp.float32
BF16 = jnp.bfloat16

EPS = 1e-6
CONV_W = 4
RG_BLOCKS = 8
RG_C = 8.0
SSD_HEAD_DIM = 64
SSD_GROUPS = 8
SSD_STATE = 128
SSD_CHUNK = 128
PEER_HEADS = 8
PEER_NKEYS = 128
PEER_TOPK = 16
LANES = 128
SUBLANES = 8
VMEM_LIMIT = 56 * 1024 * 1024
NEG_INF = float("-inf")


def _params(*sem):
    return pltpu.CompilerParams(dimension_semantics=sem, vmem_limit_bytes=VMEM_LIMIT)


def _silu(x):
    return x * jax.nn.sigmoid(x)


def _softplus(x):
    return jnp.maximum(x, 0.0) + jnp.log1p(jnp.exp(-jnp.abs(x)))


def _rms(x):
    return x * lax.rsqrt(jnp.mean(x * x, axis=-1, keepdims=True) + EPS)


def _dot(a, b):
    return jnp.dot(a, b, preferred_element_type=F32)


def _dot_nt(a, b):
    return lax.dot_general(a, b, (((1,), (1,)), ((), ())), preferred_element_type=F32)


def _dot_tn(a, b):
    return lax.dot_general(a, b, (((0,), (0,)), ((), ())), preferred_element_type=F32)


def _mod_kernel(c_ref, w_ref, b_ref, o_ref):
    cs = _silu(c_ref[...]).astype(BF16)
    o_ref[...] = _dot(cs, w_ref[...].astype(BF16)) + b_ref[...]


def _modulation(c_all, w_mod, b_mod, chunk):
    depth, d, n = w_mod.shape
    rows = c_all.shape[0]
    tn = 512
    per = chunk // tn
    return pl.pallas_call(
        _mod_kernel,
        grid=(depth, n // tn),
        in_specs=[
            pl.BlockSpec((rows, d), lambda l, j: (0, 0)),
            pl.BlockSpec((None, d, tn), lambda l, j: (l, 0, j)),
            pl.BlockSpec((None, 1, tn), lambda l, j: (l, 0, j)),
        ],
        out_specs=pl.BlockSpec((None, None, rows, tn), lambda l, j: (l, j // per, 0, j % per)),
        out_shape=jax.ShapeDtypeStruct((depth, n // chunk, rows, chunk), F32),
        compiler_params=_params("arbitrary", "arbitrary"),
        name="modulation",
    )(c_all, w_mod, b_mod.reshape(depth, 1, n))


def _proj_kernel(*refs, pro, epi, emit_lhs, n_extra, out_dtype):
    it = iter(refs)
    if pro == "normmod":
        x_ref, g_ref, sc_ref, sh_ref = next(it), next(it), next(it), next(it)
    elif pro == "gated":
        y_ref, z_ref, g_ref = next(it), next(it), next(it)
    else:
        x_ref = next(it)
    w_ref = next(it)
    b_ref = next(it) if epi in ("bias", "resid") else None
    if epi == "resid":
        xres_ref, gate_ref = next(it), next(it)
    wx_ref = next(it) if n_extra else None
    o_ref = next(it)
    lhs_out_ref = next(it) if emit_lhs else None
    ox_ref = next(it) if n_extra else None
    lhs_ref = next(it) if pro != "plain" else None

    j = pl.program_id(1)

    if pro != "plain":
        @pl.when(j == 0)
        def _():
            if pro == "normmod":
                x = x_ref[...]
                h = _rms(x) * g_ref[...] * (1.0 + sc_ref[...]) + sh_ref[...]
            else:
                y = y_ref[...]
                h = _rms(y * _silu(z_ref[...])) * g_ref[...]
            hb = h.astype(BF16)
            lhs_ref[...] = hb
            if emit_lhs:
                lhs_out_ref[...] = hb
            if n_extra:
                ox_ref[...] = _dot(hb, wx_ref[...])
        lhs = lhs_ref[...]
    else:
        lhs = x_ref[...]

    acc = _dot(lhs, w_ref[...])
    if b_ref is not None:
        acc = acc + b_ref[...]
    if epi == "resid":
        acc = xres_ref[...] + gate_ref[...] * acc
    o_ref[...] = acc.astype(out_dtype)


def _projection(pro, epi, ins, w, *, tm, tn, rows_per_mod=None, bias=None, resid=None,
                w_extra=None, emit_lhs=False, out_dtype=F32, name="proj"):
    k, n = w.shape
    m = ins[0].shape[0]
    assert m % tm == 0 and n % tn == 0
    grid = (m // tm, n // tn)

    def mod_spec(width_block, col_of):
        if rows_per_mod == 1:
            return pl.BlockSpec((tm, width_block), lambda i, j: (i, col_of(j)))
        assert rows_per_mod % tm == 0
        t = rows_per_mod // tm
        return pl.BlockSpec((None, 1, width_block), lambda i, j: (i // t, 0, col_of(j)))

    def mod_arr(a):
        return a if rows_per_mod == 1 else a.reshape(a.shape[0], 1, a.shape[1])

    args, specs = [], []
    if pro == "normmod":
        x, g, sc, sh = ins
        args += [x, g.reshape(1, k), mod_arr(sc), mod_arr(sh)]
        specs += [pl.BlockSpec((tm, k), lambda i, j: (i, 0)),
                  pl.BlockSpec((1, k), lambda i, j: (0, 0)),
                  mod_spec(k, lambda j: 0), mod_spec(k, lambda j: 0)]
    elif pro == "gated":
        y, z, g = ins
        args += [y, z, g.reshape(1, k)]
        specs += [pl.BlockSpec((tm, k), lambda i, j: (i, 0)),
                  pl.BlockSpec((tm, k), lambda i, j: (i, 0)),
                  pl.BlockSpec((1, k), lambda i, j: (0, 0))]
    else:
        (x,) = ins
        args += [x]
        specs += [pl.BlockSpec((tm, k), lambda i, j: (i, 0))]
    args.append(w)
    specs.append(pl.BlockSpec((k, tn), lambda i, j: (0, j)))
    if epi in ("bias", "resid"):
        b = bias if bias is not None else jnp.zeros((n,), F32)
        args.append(b.reshape(1, n))
        specs.append(pl.BlockSpec((1, tn), lambda i, j: (0, j)))
    if epi == "resid":
        xres, gate = resid
        args += [xres, mod_arr(gate)]
        specs += [pl.BlockSpec((tm, tn), lambda i, j: (i, j)), mod_spec(tn, lambda j: j)]
    n_extra = 0
    if w_extra is not None:
        n_extra = w_extra.shape[1]
        args.append(w_extra)
        specs.append(pl.BlockSpec((k, n_extra), lambda i, j: (0, 0)))

    out_shapes = [jax.ShapeDtypeStruct((m, n), out_dtype)]
    out_specs = [pl.BlockSpec((tm, tn), lambda i, j: (i, j))]
    if emit_lhs:
        out_shapes.append(jax.ShapeDtypeStruct((m, k), BF16))
        out_specs.append(pl.BlockSpec((tm, k), lambda i, j: (i, 0)))
    if n_extra:
        out_shapes.append(jax.ShapeDtypeStruct((m, n_extra), F32))
        out_specs.append(pl.BlockSpec((tm, n_extra), lambda i, j: (i, 0)))
    scratch = [pltpu.VMEM((tm, k), BF16)] if pro != "plain" else []

    outs = pl.pallas_call(
        functools.partial(_proj_kernel, pro=pro, epi=epi, emit_lhs=emit_lhs, n_extra=n_extra,
                          out_dtype=out_dtype),
        grid=grid, in_specs=specs, out_specs=out_specs, out_shape=out_shapes,
        scratch_shapes=scratch,
        compiler_params=_params("arbitrary", "arbitrary"),
        name=name,
    )(*args)
    return outs if len(outs) > 1 else outs[0]


def _rglru_gates(xc, wa_ref, ba_ref, wi_ref, bi_ref, lam_ref):
    xcb = xc.astype(BF16)
    r = jax.nn.sigmoid(_dot(xcb, wa_ref[...]) + ba_ref[...])
    i = jax.nn.sigmoid(_dot(xcb, wi_ref[...]) + bi_ref[...])
    log_a = (-RG_C * _softplus(-lam_ref[...])) * r
    a = jnp.exp(log_a)
    mult = jnp.sqrt(jnp.maximum(-jnp.tanh(log_a) * (a * a + 1.0), 0.0))
    return a, mult * (i * xc)


RG_ROWS = 256


def _rglru_seq_kernel(gate_ref, xr_ref, cw_ref, cb_ref, wa_ref, ba_ref, wi_ref, bi_ref, lam_ref,
                      y_ref, hlast_ref, xpad_ref):
    seq, blk = xr_ref.shape
    pad = SUBLANES
    xpad_ref[0:pad, :] = jnp.zeros((pad, blk), F32)
    xpad_ref[pad:pad + seq, :] = xr_ref[...]
    row_in_tile = lax.broadcasted_iota(jnp.int32, (RG_ROWS, blk), 0) % SUBLANES

    def step(c, carry):
        start = pl.multiple_of(c * RG_ROWS, RG_ROWS)
        xe = xpad_ref[pl.ds(start, RG_ROWS + pad), :]
        xc = cb_ref[...] + cw_ref[3:4, :] * xe[pad:]
        for k in range(1, CONV_W):
            xc = xc + cw_ref[3 - k:4 - k, :] * pltpu.roll(xe, k, axis=0)[pad:]
        a, b = _rglru_gates(xc, wa_ref, ba_ref, wi_ref, bi_ref, lam_ref)
        for s in (1, 2, 4):
            m = row_in_tile >= s
            b = jnp.where(m, a * pltpu.roll(b, s, axis=0) + b, b)
            a = jnp.where(m, a * pltpu.roll(a, s, axis=0), a)
        hs = []
        for t in range(RG_ROWS // SUBLANES):
            sl = slice(t * SUBLANES, (t + 1) * SUBLANES)
            h = b[sl] + a[sl] * carry
            hs.append(h)
            carry = h[SUBLANES - 1:SUBLANES, :]
        h_all = jnp.concatenate(hs, axis=0)
        y_ref[pl.ds(start, RG_ROWS), :] = (h_all * jax.nn.gelu(gate_ref[pl.ds(start, RG_ROWS), :])).astype(BF16)
        return carry

    carry = lax.fori_loop(0, seq // RG_ROWS, step, jnp.zeros((1, blk), F32))
    hlast_ref[...] = carry


def _rglru_seq(proj, conv_w, conv_b, w_a, b_a, w_i, b_i, lam):
    nb, seq, two_d = proj.shape
    d = two_d // 2
    blk = d // RG_BLOCKS
    vec = lambda a: a.reshape(1, d)
    vspec = pl.BlockSpec((1, blk), lambda b, k: (0, k))
    wspec = pl.BlockSpec((None, blk, blk), lambda b, k: (k, 0, 0))
    return pl.pallas_call(
        _rglru_seq_kernel,
        grid=(nb, RG_BLOCKS),
        in_specs=[
            pl.BlockSpec((None, seq, blk), lambda b, k: (b, 0, k)),
            pl.BlockSpec((None, seq, blk), lambda b, k: (b, 0, RG_BLOCKS + k)),
            pl.BlockSpec((CONV_W, blk), lambda b, k: (0, k)),
            vspec, wspec, vspec, wspec, vspec, vspec,
        ],
        out_specs=[
            pl.BlockSpec((None, seq, blk), lambda b, k: (b, 0, k)),
            pl.BlockSpec((None, 1, blk), lambda b, k: (b, 0, k)),
        ],
        out_shape=[jax.ShapeDtypeStruct((nb, seq, d), BF16), jax.ShapeDtypeStruct((nb, 1, d), F32)],
        scratch_shapes=[pltpu.VMEM((seq + SUBLANES, blk), F32)],
        compiler_params=_params("arbitrary", "arbitrary"),
        name="rglru_seq",
    )(proj, proj, conv_w, vec(conv_b), w_a.astype(BF16), vec(b_a), w_i.astype(BF16), vec(b_i), vec(lam))


def _rglru_step_kernel(gate_ref, xr_ref, c0_ref, c1_ref, c2_ref, h0_ref, cw_ref, cb_ref,
                       wa_ref, ba_ref, wi_ref, bi_ref, lam_ref, y_ref, h_ref):
    xc = (cb_ref[...] + cw_ref[0:1, :] * c0_ref[...] + cw_ref[1:2, :] * c1_ref[...]
          + cw_ref[2:3, :] * c2_ref[...] + cw_ref[3:4, :] * xr_ref[...])
    a, b = _rglru_gates(xc, wa_ref, ba_ref, wi_ref, bi_ref, lam_ref)
    h = a * h0_ref[...] + b
    h_ref[...] = h
    y_ref[...] = (h * jax.nn.gelu(gate_ref[...])).astype(BF16)


def _rglru_step(proj, conv_state, h0, conv_w, conv_b, w_a, b_a, w_i, b_i, lam):
    nb, two_d = proj.shape
    d = two_d // 2
    blk = d // RG_BLOCKS
    vec = lambda a: a.reshape(1, d)
    rows = pl.BlockSpec((nb, blk), lambda k: (0, k))
    vspec = pl.BlockSpec((1, blk), lambda k: (0, k))
    wspec = pl.BlockSpec((None, blk, blk), lambda k: (k, 0, 0))
    return pl.pallas_call(
        _rglru_step_kernel,
        grid=(RG_BLOCKS,),
        in_specs=[rows, pl.BlockSpec((nb, blk), lambda k: (0, RG_BLOCKS + k)), rows, rows, rows, rows,
                  pl.BlockSpec((CONV_W, blk), lambda k: (0, k)),
                  vspec, wspec, vspec, wspec, vspec, vspec],
        out_specs=[rows, rows],
        out_shape=[jax.ShapeDtypeStruct((nb, d), BF16), jax.ShapeDtypeStruct((nb, d), F32)],
        compiler_params=_params("arbitrary"),
        name="rglru_step",
    )(proj, proj, conv_state[:, 0], conv_state[:, 1], conv_state[:, 2], h0, conv_w, vec(conv_b),
      w_a.astype(BF16), vec(b_a), w_i.astype(BF16), vec(b_i), vec(lam))


def _split3(x):
    hi = x.astype(BF16)
    r1 = x - hi.astype(F32)
    mid = r1.astype(BF16)
    lo = (r1 - mid.astype(F32)).astype(BF16)
    return hi, mid, lo


def _head_expand(x, expand_ref):
    hi = x.astype(BF16)
    lo = (x - hi.astype(F32)).astype(BF16)
    return _dot(hi, expand_ref[...]) + _dot(lo, expand_ref[...])


def _ssd_seq_kernel(xbc_ref, halo_ref, dt_ref, z_ref, cw_ref, cb_ref, dtb_ref, alog_ref, drep_ref, g_ref,
                    expand_ref, yn_ref, st_ref, *, d_inner):
    c = pl.program_id(1)
    q = SSD_CHUNK
    p = SSD_HEAD_DIM
    n = SSD_STATE
    hpg = d_inner // p // SSD_GROUPS
    epp = hpg * p
    pad = SUBLANES

    @pl.when(c == 0)
    def _():
        st_ref[...] = jnp.zeros(st_ref.shape, F32)

    halo = jnp.where(c > 0, halo_ref[...], 0.0)
    xe = jnp.concatenate([halo, xbc_ref[...]], axis=0)
    xc = cb_ref[...] + cw_ref[3:4, :] * xe[pad:]
    for k in range(1, CONV_W):
        xc = xc + cw_ref[3 - k:4 - k, :] * pltpu.roll(xe, k, axis=0)[pad:]
    xc = _silu(xc)
    xs = xc[:, :d_inner]

    dt = _softplus(dt_ref[...] + dtb_ref[...])
    da = dt * (-jnp.exp(alog_ref[...]))
    ri = lax.broadcasted_iota(jnp.int32, (q, q), 0)
    ci = lax.broadcasted_iota(jnp.int32, (q, q), 1)
    tri = ri >= ci
    ltri = jnp.where(tri, 1.0, 0.0).astype(BF16)
    hi, mid, lo = _split3(da)
    cum = _dot(ltri, hi) + _dot(ltri, mid) + _dot(ltri, lo)
    cum_t = cum.T
    dt_t = dt.T
    last = cum[q - 1:q, :]
    ecum_x = _head_expand(jnp.exp(cum), expand_ref)
    wend_x = _head_expand(jnp.exp(last - cum) * dt, expand_ref)
    elast_x = _head_expand(jnp.broadcast_to(jnp.exp(last), (SUBLANES, last.shape[1])), expand_ref)[0:1, :]
    xw = (xs * wend_x).astype(BF16)
    first_head = lax.broadcasted_iota(jnp.int32, (q, 2 * p), 1) < p

    y_groups = []
    for g in range(SSD_GROUPS):
        bb = xc[:, d_inner + g * n:d_inner + (g + 1) * n].astype(BF16)
        cc = xc[:, d_inner + SSD_GROUPS * n + g * n:d_inner + SSD_GROUPS * n + (g + 1) * n].astype(BF16)
        cb = _dot_nt(cc, bb)
        st = st_ref[g]
        ys = _dot(cc, st.astype(BF16))
        pairs = []
        for e in range(0, hpg, 2):
            xpair = xs[:, (g * hpg + e) * p:(g * hpg + e + 2) * p].astype(BF16)
            acc = None
            for k in range(2):
                h = g * hpg + e + k
                decay = jnp.exp(jnp.where(tri, cum[:, h:h + 1] - cum_t[h:h + 1, :], NEG_INF))
                w = (decay * cb * dt_t[h:h + 1, :]).astype(BF16)
                keep = first_head if k == 0 else jnp.logical_not(first_head)
                term = _dot(w, jnp.where(keep, xpair, jnp.zeros((), BF16)))
                acc = term if acc is None else acc + term
            pairs.append(acc)
        lanes = slice(g * epp, (g + 1) * epp)
        y_groups.append(jnp.concatenate(pairs, axis=1) + ys * ecum_x[:, lanes])
        st_ref[g] = st * elast_x[:, lanes] + _dot_tn(bb, xw[:, lanes])
    y = jnp.concatenate(y_groups, axis=1) + drep_ref[...] * xs
    yn_ref[...] = (_rms(y * _silu(z_ref[...])) * g_ref[...]).astype(BF16)


def _ssd_seq(xbc, dt_raw, z, conv_w, conv_b, dt_bias, a_log, d_skip, norm_g, d_inner):
    nb, seq, cdim = xbc.shape
    q = SSD_CHUNK
    hp = dt_raw.shape[-1]
    heads = d_inner // SSD_HEAD_DIM
    epp = d_inner // SSD_GROUPS
    padv = lambda a: jnp.pad(a.reshape(1, heads), ((0, 0), (0, hp - heads)))
    halo_blocks = q // SUBLANES
    expand = (jnp.arange(d_inner)[None, :] // SSD_HEAD_DIM == jnp.arange(hp)[:, None]).astype(BF16)
    const = lambda shape: pl.BlockSpec(shape, lambda b, c: (0, 0))
    return pl.pallas_call(
        functools.partial(_ssd_seq_kernel, d_inner=d_inner),
        grid=(nb, seq // q),
        in_specs=[
            pl.BlockSpec((None, q, cdim), lambda b, c: (b, c, 0)),
            pl.BlockSpec((None, SUBLANES, cdim), lambda b, c: (b, jnp.maximum(c * halo_blocks - 1, 0), 0)),
            pl.BlockSpec((None, q, hp), lambda b, c: (b, c, 0)),
            pl.BlockSpec((None, q, d_inner), lambda b, c: (b, c, 0)),
            const((CONV_W, cdim)), const((1, cdim)), const((1, hp)), const((1, hp)),
            const((1, d_inner)), const((1, d_inner)), const((hp, d_inner)),
        ],
        out_specs=[
            pl.BlockSpec((None, q, d_inner), lambda b, c: (b, c, 0)),
            pl.BlockSpec((None, SSD_GROUPS, SSD_STATE, epp), lambda b, c: (b, 0, 0, 0)),
        ],
        out_shape=[jax.ShapeDtypeStruct((nb, seq, d_inner), BF16),
                   jax.ShapeDtypeStruct((nb, SSD_GROUPS, SSD_STATE, epp), F32)],
        compiler_params=_params("arbitrary", "arbitrary"),
        name="ssd_seq",
    )(xbc, xbc, dt_raw, z, conv_w, conv_b.reshape(1, cdim), padv(dt_bias), padv(a_log),
      jnp.repeat(d_skip, SSD_HEAD_DIM).reshape(1, d_inner), norm_g.reshape(1, d_inner), expand)


def _ssd_step_pre_kernel(xbc_ref, c0_ref, c1_ref, c2_ref, dt_ref, cw_ref, cb_ref, dtb_ref, alog_ref, drep_ref,
                         xdt_ref, skip_ref, bc_ref, dec_ref, *, d_inner):
    xc = (cb_ref[...] + cw_ref[0:1, :] * c0_ref[...] + cw_ref[1:2, :] * c1_ref[...]
          + cw_ref[2:3, :] * c2_ref[...] + cw_ref[3:4, :] * xbc_ref[...])
    xc = _silu(xc)
    dt = _softplus(dt_ref[...] + dtb_ref[...])
    dec_ref[...] = jnp.exp(dt * (-jnp.exp(alog_ref[...])))
    hp = dt.shape[1]
    head_of_lane = lax.broadcasted_iota(jnp.int32, (hp, d_inner), 1) // SSD_HEAD_DIM
    expand = jnp.where(head_of_lane == lax.broadcasted_iota(jnp.int32, (hp, d_inner), 0), 1.0, 0.0).astype(BF16)
    dt_rep = sum(_dot(piece, expand) for piece in _split3(dt))
    xs = xc[:, :d_inner]
    xdt_ref[...] = xs * dt_rep
    skip_ref[...] = xs * drep_ref[...]
    bc_ref[...] = xc[:, d_inner:]


def _ssd_step_state_kernel(dec_ref, s_ref, xdt_ref, skip_ref, b_ref, c_ref, s_out_ref, y_ref):
    tb, npair, rows, n = s_ref.shape
    pairs_per_group = npair // SSD_GROUPS
    seq0 = pl.program_id(0) * tb
    row_id = lax.broadcasted_iota(jnp.int32, (rows, n), 0)
    pair_id = lax.broadcasted_iota(jnp.int32, (npair, n), 0)
    for b in range(tb):
        xt = xdt_ref[b].T.astype(BF16)
        bm = b_ref[b]
        brep = jnp.concatenate([jnp.broadcast_to(bm[g:g + 1], (pairs_per_group, n))
                                for g in range(SSD_GROUPS)], axis=0)
        bbig = jnp.concatenate([jnp.where(pair_id == j, brep, 0.0) for j in range(npair)], axis=1)
        upd = _dot(xt, bbig.astype(BF16))
        cm = c_ref[b].astype(BF16)
        for j in range(npair):
            g = j // pairs_per_group
            d0 = dec_ref[seq0 + b, 2 * j]
            d1 = dec_ref[seq0 + b, 2 * j + 1]
            s_new = s_ref[b, j] * jnp.where(row_id < rows // 2, d0, d1) + upd[:, j * n:(j + 1) * n]
            s_out_ref[b, j] = s_new
            y8 = _dot_nt(cm, s_new.astype(BF16))
            y_ref[b, j:j + 1, :] = y8[g:g + 1, :] + skip_ref[b, j:j + 1, :]


def _ssd_step(xbc_raw, conv_state, dt_raw, state, conv_w, conv_b, dt_bias, a_log, d_skip, d_inner):
    nb, cdim = xbc_raw.shape
    hp = dt_raw.shape[-1]
    heads = d_inner // SSD_HEAD_DIM
    p, n = SSD_HEAD_DIM, SSD_STATE
    gn = SSD_GROUPS * n
    padv = lambda a: jnp.pad(a.reshape(1, heads), ((0, 0), (0, hp - heads)))
    full = lambda shape: pl.BlockSpec(shape, lambda: tuple(0 for _ in shape))
    xdt, skip, bc, dec = pl.pallas_call(
        functools.partial(_ssd_step_pre_kernel, d_inner=d_inner),
        in_specs=[full((nb, cdim))] * 4 + [full((nb, hp)), full((CONV_W, cdim)), full((1, cdim)),
                                            full((1, hp)), full((1, hp)), full((1, d_inner))],
        out_specs=[full((nb, d_inner)), full((nb, d_inner)), full((nb, 2 * gn)), full((nb, hp))],
        out_shape=[jax.ShapeDtypeStruct((nb, d_inner), F32), jax.ShapeDtypeStruct((nb, d_inner), F32),
                   jax.ShapeDtypeStruct((nb, 2 * gn), F32), jax.ShapeDtypeStruct((nb, hp), F32)],
        compiler_params=pltpu.CompilerParams(vmem_limit_bytes=VMEM_LIMIT),
        name="ssd_step_pre",
    )(xbc_raw, conv_state[:, 0], conv_state[:, 1], conv_state[:, 2], dt_raw, conv_w,
      conv_b.reshape(1, cdim), padv(dt_bias), padv(a_log), jnp.repeat(d_skip, p).reshape(1, d_inner))
    npair = heads // 2
    pair3 = lambda a: a.reshape(nb, npair, 2 * p)
    tb = 2
    sspec = pl.BlockSpec((tb, npair, 2 * p, n), lambda i: (i, 0, 0, 0))
    vspec = pl.BlockSpec((tb, npair, 2 * p), lambda i: (i, 0, 0))
    gspec = pl.BlockSpec((tb, SSD_GROUPS, n), lambda i: (i, 0, 0))
    s_new, y3 = pl.pallas_call(
        _ssd_step_state_kernel,
        grid=(nb // tb,),
        in_specs=[pl.BlockSpec(memory_space=pltpu.SMEM), sspec, vspec, vspec, gspec, gspec],
        out_specs=[sspec, vspec],
        out_shape=[jax.ShapeDtypeStruct((nb, npair, 2 * p, n), F32),
                   jax.ShapeDtypeStruct((nb, npair, 2 * p), F32)],
        compiler_params=_params("arbitrary"),
        name="ssd_step_state",
    )(dec, state.reshape(nb, npair, 2 * p, n), pair3(xdt), pair3(skip),
      bc[:, :gn].reshape(nb, SSD_GROUPS, n), bc[:, gn:].reshape(nb, SSD_GROUPS, n))
    return y3.reshape(nb, d_inner), s_new


NOT_RANKED = 99.0


def _tree(op, xs):
    xs = list(xs)
    while len(xs) > 1:
        xs = [op(xs[i], xs[i + 1]) if i + 1 < len(xs) else xs[i] for i in range(0, len(xs), 2)]
    return xs[0]
PEER_PAIRS = [(a, b) for a in range(PEER_TOPK) for b in range(PEER_TOPK) if (a + 1) * (b + 1) <= PEER_TOPK]


def _sort_network(n):
    pairs = []
    p = 1
    while p < n:
        k = p
        while k >= 1:
            for j in range(k % p, n - k, 2 * k):
                for i in range(min(k, n - j - k)):
                    if (i + j) // (2 * p) == (i + j + k) // (2 * p):
                        pairs.append((i + j, i + j + k))
            k //= 2
        p *= 2
    return pairs


SORT16 = _sort_network(PEER_TOPK)
BITONIC16 = [(i, i + st) for st in (8, 4, 2, 1) for i in range(PEER_TOPK) if (i // st) % 2 == 0]


def _exchange(xs, pairs):
    for i, j in pairs:
        xs[i], xs[j] = jnp.maximum(xs[i], xs[j]), jnp.minimum(xs[i], xs[j])
    return xs


def _top16_values(w):
    groups = [_exchange(list(w[g:g + PEER_TOPK]), SORT16) for g in range(0, len(w), PEER_TOPK)]
    dropped = []
    while len(groups) > 1:
        merged = []
        for a, b in zip(groups[0::2], groups[1::2]):
            hi = [jnp.maximum(a[i], b[PEER_TOPK - 1 - i]) for i in range(PEER_TOPK)]
            dropped.append(_tree(jnp.maximum, [jnp.minimum(a[i], b[PEER_TOPK - 1 - i]) for i in range(PEER_TOPK)]))
            merged.append(_exchange(hi, BITONIC16))
        groups = merged
    return groups[0], _tree(jnp.maximum, dropped)


def _staircase(v1, v2, shape):
    cand = {c: v1[c[0]] + v2[c[1]] for c in PEER_PAIRS}
    ahead_of = {c: [] for c in PEER_PAIRS}
    behind_of = {c: [] for c in PEER_PAIRS}
    for n1, c in enumerate(PEER_PAIRS):
        for c2 in PEER_PAIRS[n1 + 1:]:
            if c2[0] >= c[0] and c2[1] >= c[1]:
                continue
            ahead = jnp.where(cand[c] >= cand[c2], 1.0, 0.0)
            ahead_of[c2].append(ahead)
            behind_of[c].append(ahead)
    e1v = [jnp.exp(v1[a] - v1[0]) for a in range(PEER_TOPK)]
    e2v = [jnp.exp(v2[b] - v2[0]) for b in range(PEER_TOPK)]
    keep_terms = [[] for _ in range(PEER_TOPK)]
    z_terms = []
    for c in PEER_PAIRS:
        cnt = jnp.full(shape, float((c[0] + 1) * (c[1] + 1) - 1 + len(behind_of[c])), F32)
        if ahead_of[c]:
            cnt = cnt + _tree(jnp.add, ahead_of[c])
        if behind_of[c]:
            cnt = cnt - _tree(jnp.add, behind_of[c])
        keep = jnp.where(cnt < float(PEER_TOPK), 1.0, 0.0)
        keep_terms[c[0]].append(keep)
        z_terms.append(keep * (e1v[c[0]] * e2v[c[1]]))
    return [_tree(jnp.add, t) for t in keep_terms], 1.0 / _tree(jnp.add, z_terms)


def _count_true(pred, ts):
    conds, total = [], None
    n = len(ts)
    step = n // 2
    while step >= 1:
        def probe(cs, lo, width):
            if not cs:
                return ts[lo + step - 1]
            half = width // 2
            return jnp.where(cs[0], probe(cs[1:], lo + half, half), probe(cs[1:], lo, half))
        c = pred(probe(conds, 0, n))
        conds.append(c)
        term = jnp.where(c, float(step), 0.0)
        total = term if total is None else total + term
        step //= 2
    return total + jnp.where(pred(ts[n - 1]), 1.0, 0.0)


def _peer_prep_kernel(q_ref, k1_ref, k2_ref, r2_ref, e2_ref, c1_ref, e1_ref,
                      s_scr, work_scr, rank_scr, v_scr):
    tt = q_ref.shape[0]
    hh = PEER_HEADS
    half = k1_ref.shape[1]
    tile = lambda k: slice(k * hh, (k + 1) * hh)
    twice = lambda a: jnp.concatenate([a, a], axis=0)

    for side, k_ref in enumerate((k1_ref, k2_ref)):
        s_scr[side] = _dot_nt(k_ref[...], q_ref[:, side * half:(side + 1) * half])

    tops, ties = [], []
    for side in range(2):
        v, nxt = _top16_values([s_scr[side, tile(k), :] for k in range(PEER_NKEYS)])
        tops.append(v)
        ties += [jnp.where(v[a] == v[a + 1], 1.0, 0.0) for a in range(PEER_TOPK - 1)]
        ties.append(jnp.where(v[PEER_TOPK - 1] == nxt, 1.0, 0.0))
    v1, v2 = tops
    ncols, rz = _staircase(v1, v2, (hh, tt))
    inf = jnp.full((hh, tt), float("inf"), F32)
    thr = [_tree(jnp.minimum, [jnp.where(ncols[a] > float(m), v1[a], inf)
                               for a in range(PEER_TOPK) if (a + 1) * (m + 1) <= PEER_TOPK])
           for m in range(PEER_TOPK)]
    thr2 = [twice(t) for t in thr]
    v2x = [twice(v) for v in v2]
    m1, m2, rz2 = twice(v1[0]), twice(v2[0]), twice(rz)
    for k in range(0, PEER_NKEYS, 2):
        rows = slice(k * hh, (k + 2) * hh)
        s1 = s_scr[0, rows, :]
        s2 = s_scr[1, rows, :]
        c1_ref[rows, :] = _count_true(lambda t: s1 >= t, thr2)
        r2_ref[rows, :] = _count_true(lambda v: v > s2, v2x).astype(BF16)
        e1_ref[rows, :] = jnp.exp(s1 - m1) * rz2
        e2_ref[rows, :] = jnp.exp(s2 - m2).astype(BF16)

    @pl.when(jnp.max(_tree(jnp.maximum, ties)) > 0.0)
    def _():
        for side in range(2):
            work_scr[side] = s_scr[side]
            rank_scr[side] = jnp.full(work_scr.shape[1:], NOT_RANKED, F32)

        def extract(a, _):
            af = jnp.asarray(a, F32)
            for side in range(2):
                w = [work_scr[side, tile(k), :] for k in range(PEER_NKEYS)]
                m = _tree(jnp.maximum, w)
                sel = _tree(jnp.minimum, [jnp.where(w[k] == m, k, PEER_NKEYS) for k in range(PEER_NKEYS)])
                for k in range(PEER_NKEYS):
                    hit = sel == k
                    work_scr[side, tile(k), :] = jnp.where(hit, NEG_INF, w[k])
                    rank_scr[side, tile(k), :] = jnp.where(hit, af, rank_scr[side, tile(k), :])
                v_scr[side, a] = m
            return 0

        lax.fori_loop(0, PEER_TOPK, extract, 0)
        u1 = [v_scr[0, a] for a in range(PEER_TOPK)]
        u2 = [v_scr[1, b] for b in range(PEER_TOPK)]
        ncols_t, rz_t = _staircase(u1, u2, (hh, tt))
        ncols2 = [twice(c) for c in ncols_t]
        rzt2 = twice(rz_t)
        for k in range(0, PEER_NKEYS, 2):
            rows = slice(k * hh, (k + 2) * hh)
            r1 = rank_scr[0, rows, :]
            c1 = jnp.zeros((2 * hh, tt), F32)
            for a in range(PEER_TOPK):
                c1 = jnp.where(r1 == float(a), ncols2[a], c1)
            c1_ref[rows, :] = c1
            e1_ref[rows, :] = jnp.exp(s_scr[0, rows, :] - m1) * rzt2
            r2_ref[rows, :] = rank_scr[1, rows, :].astype(BF16)


def _peer_prep(qp, k1big, k2big, tt=LANES):
    t = qp.shape[0]
    rows = k1big.shape[0]
    out = lambda dt: jax.ShapeDtypeStruct((rows, t), dt)
    ospec = pl.BlockSpec((rows, tt), lambda i: (0, i))
    return pl.pallas_call(
        _peer_prep_kernel,
        grid=(t // tt,),
        in_specs=[pl.BlockSpec((tt, qp.shape[1]), lambda i: (i, 0)),
                  pl.BlockSpec(k1big.shape, lambda i: (0, 0)),
                  pl.BlockSpec(k2big.shape, lambda i: (0, 0))],
        out_specs=[ospec] * 4,
        out_shape=[out(BF16), out(BF16), out(F32), out(F32)],
        scratch_shapes=[pltpu.VMEM((2, rows, tt), F32), pltpu.VMEM((2, rows, tt), F32),
                        pltpu.VMEM((2, rows, tt), F32), pltpu.VMEM((2, PEER_TOPK, PEER_HEADS, tt), F32)],
        compiler_params=_params("arbitrary"),
        name="peer_prep",
    )(qp, k1big, k2big)


def _peer_expert_kernel(hc_ref, u_ref, vt_ref, r2_ref, e2_ref, c1_ref, e1_ref, x_ref, gate_ref, *rest):
    *final_g_ref, o_ref, acc_scr = rest
    e = pl.program_id(1)
    te = u_ref.shape[0]
    nk = PEER_NKEYS

    @pl.when(e == 0)
    def _():
        acc_scr[...] = jnp.zeros(acc_scr.shape, F32)

    act = jax.nn.gelu(_dot_nt(u_ref[...], hc_ref[...])).astype(BF16)
    blocks = []
    for ii in range(te // nk):
        w = None
        for h in range(PEER_HEADS):
            row = ii * PEER_HEADS + h
            c1 = c1_ref[row:row + 1, :].astype(BF16)
            e1 = e1_ref[row:row + 1, :].astype(BF16)
            term = jnp.where(r2_ref[h] < c1, e2_ref[h], jnp.zeros((), BF16)) * e1
            w = term if w is None else w + term
        blocks.append(act[ii * nk:(ii + 1) * nk] * w)
    acc_scr[...] += _dot(vt_ref[...], jnp.concatenate(blocks, axis=0))

    @pl.when(e == pl.num_programs(1) - 1)
    def _():
        y = x_ref[...] + gate_ref[...] * acc_scr[...].T
        o_ref[...] = _rms(y) * final_g_ref[0][...] if final_g_ref else y


def _peer_expert(hc, u_bf, vt_bf, layer, r2, e2, c1, e1, x, gate, *, tm, te, rows_per_mod, final_g=None):
    t, d = hc.shape
    n_exp = u_bf.shape[1]
    rows_i = te // PEER_NKEYS * PEER_HEADS
    hh, nk = PEER_HEADS, PEER_NKEYS
    once = pl.Buffered(1)
    if rows_per_mod == 1:
        gate_arr = gate
        gate_spec = pl.BlockSpec((tm, d), lambda i, e: (i, 0), pipeline_mode=once)
    else:
        per = rows_per_mod // tm
        gate_arr = gate.reshape(gate.shape[0], 1, d)
        gate_spec = pl.BlockSpec((None, 1, d), lambda i, e: (i // per, 0, 0))
    return pl.pallas_call(
        _peer_expert_kernel,
        grid=(t // tm, n_exp // te),
        in_specs=[
            pl.BlockSpec((tm, d), lambda i, e: (i, 0), pipeline_mode=once),
            pl.BlockSpec((None, te, d), lambda i, e: (layer, e, 0)),
            pl.BlockSpec((None, d, te), lambda i, e: (layer, 0, e)),
            pl.BlockSpec((hh, nk, tm), lambda i, e: (0, 0, i), pipeline_mode=once),
            pl.BlockSpec((hh, nk, tm), lambda i, e: (0, 0, i), pipeline_mode=once),
            pl.BlockSpec((rows_i, tm), lambda i, e: (e, i)),
            pl.BlockSpec((rows_i, tm), lambda i, e: (e, i)),
            pl.BlockSpec((tm, d), lambda i, e: (i, 0), pipeline_mode=once),
            gate_spec,
        ] + ([pl.BlockSpec((1, d), lambda i, e: (0, 0))] if final_g is not None else []),
        out_specs=pl.BlockSpec((tm, d), lambda i, e: (i, 0)),
        out_shape=jax.ShapeDtypeStruct((t, d), F32),
        scratch_shapes=[pltpu.VMEM((d, tm), F32)],
        compiler_params=_params("arbitrary", "arbitrary"),
        name="peer_expert",
    )(hc, u_bf, vt_bf, r2, e2, c1, e1, x, gate_arr, *([final_g.reshape(1, d)] if final_g is not None else []))


def _transpose_cast_kernel(x_ref, o_ref):
    o_ref[...] = x_ref[...].T.astype(BF16)


def _transpose_cast(x, tr):
    depth, rows, cols = x.shape
    return pl.pallas_call(
        _transpose_cast_kernel,
        grid=(depth, rows // tr),
        in_specs=[pl.BlockSpec((None, tr, cols), lambda l, i: (l, i, 0))],
        out_specs=pl.BlockSpec((None, cols, tr), lambda l, i: (l, 0, i)),
        out_shape=jax.ShapeDtypeStruct((depth, cols, rows), BF16),
        compiler_params=_params("arbitrary", "arbitrary"),
        name="transpose_cast",
    )(x)


TM_PROMPT = 1024
TN_PROJ = 1024
TM_EXPERT = 512
TE_EXPERT = 1024
TE_EXPERT_SAMPLE = 2048


def _peer_layer(groups, w_q, k1, k2, u_bf, vt_bf, layer, norm_g, final_g=None):
    d = w_q.shape[0]
    hh, nk = PEER_HEADS, PEER_NKEYS
    half = k1.shape[1]
    wq = w_q.reshape(d, hh, 2, half).transpose(0, 2, 1, 3).reshape(d, 2 * hh * half).astype(BF16)
    eye = jnp.eye(hh, dtype=F32)
    kbig = lambda k: jnp.einsum("kd,hg->khgd", k, eye).reshape(nk * hh, hh * half).astype(BF16)
    k1b, k2b = kbig(k1), kbig(k2)
    outs = []
    for gr in groups:
        x = gr["x"]
        t = x.shape[0]
        qp, hc = _projection("normmod", "none", (x, norm_g, gr["sc"], gr["sh"]), wq,
                             tm=gr["tm"], tn=TN_PROJ, rows_per_mod=gr["rows_per_mod"],
                             emit_lhs=True, out_dtype=BF16, name="peer_q")
        r2, e2, c1, e1 = _peer_prep(qp, k1b, k2b)
        r2, e2 = [a.reshape(nk, hh, t).transpose(1, 0, 2) for a in (r2, e2)]
        outs.append(_peer_expert(hc, u_bf, vt_bf, layer, r2, e2, c1, e1, x, gr["gate"],
                                 tm=gr["tme"], te=gr["te"], rows_per_mod=gr["rows_per_mod"], final_g=final_g))
    return outs


def kernel(x_prompt, x_sample, state_rg_conv, state_rg_h, state_ssd_conv, state_ssd, c_prompt, c_sample, norm1_g, norm2_g, w_mod, b_mod, rg_w_in, rg_b_in, rg_conv_w, rg_conv_b, rg_w_a, rg_b_a, rg_w_i, rg_b_i, rg_lambda, rg_w_out, rg_b_out, ssd_w_in, ssd_conv_w, ssd_conv_b, ssd_dt_bias, ssd_a_log, ssd_d, ssd_norm_g, ssd_w_out, peer_w_q, peer_k1, peer_k2, peer_u, peer_v, final_g):
    bp, seq, d = x_prompt.shape
    bs = x_sample.shape[0]
    tp = bp * seq
    depth = w_mod.shape[0]
    d_rnn = rg_w_out.shape[1]
    d_inner = ssd_w_out.shape[1]
    cdim = ssd_conv_w.shape[-1]
    heads = d_inner // SSD_HEAD_DIM

    nc = bp + bs
    c_all = jnp.concatenate([c_sample, c_prompt, jnp.zeros((-nc % SUBLANES, d), F32)], axis=0)
    mod = _modulation(c_all, w_mod, b_mod, d)
    u_bf = peer_u.astype(BF16)
    vt_bf = _transpose_cast(peer_v, 512)

    xp = x_prompt.reshape(tp, d)
    xs = x_sample.reshape(bs, d)
    tms = bs
    outs = {}

    for l in range(depth):
        mp = [mod[l, i, bs:nc] for i in range(6)]
        ms = [mod[l, i, :bs] for i in range(6)]
        j = l // 2
        if l % 2 == 0:
            w_in = rg_w_in[j].astype(BF16)
            w_out = rg_w_out[j].astype(BF16)
            rg = (rg_conv_w[j], rg_conv_b[j], rg_w_a[j], rg_b_a[j], rg_w_i[j], rg_b_i[j], rg_lambda[j])
            proj_p = _projection("normmod", "bias", (xp, norm1_g[l], mp[1], mp[0]), w_in, tm=TM_PROMPT,
                                 tn=TN_PROJ, rows_per_mod=seq, bias=rg_b_in[j], name="rg_in")
            proj_s = _projection("normmod", "bias", (xs, norm1_g[l], ms[1], ms[0]), w_in, tm=tms,
                                 tn=TN_PROJ, rows_per_mod=1, bias=rg_b_in[j], name="rg_in")
            proj_p3 = proj_p.reshape(bp, seq, 2 * d_rnn)
            y_p, h_p = _rglru_seq(proj_p3, *rg)
            y_s, h_s = _rglru_step(proj_s, state_rg_conv[j], state_rg_h[j], *rg)
            xp = _projection("plain", "resid", (y_p.reshape(tp, d_rnn),), w_out, tm=TM_PROMPT, tn=TN_PROJ,
                             rows_per_mod=seq, bias=rg_b_out[j], resid=(xp, mp[2]), name="rg_out")
            xs = _projection("plain", "resid", (y_s,), w_out, tm=tms, tn=TN_PROJ,
                             rows_per_mod=1, bias=rg_b_out[j], resid=(xs, ms[2]), name="rg_out")
            outs.setdefault("rg_conv_p", []).append(proj_p3[:, seq - (CONV_W - 1):, d_rnn:])
            outs.setdefault("rg_h_p", []).append(h_p.reshape(bp, d_rnn))
            outs.setdefault("rg_conv_s", []).append(
                jnp.concatenate([state_rg_conv[j][:, 1:], proj_s[:, None, d_rnn:]], axis=1))
            outs.setdefault("rg_h_s", []).append(h_s)
        else:
            w_in = ssd_w_in[j]
            w_z = w_in[:, :d_inner].astype(BF16)
            w_xbc = w_in[:, d_inner:d_inner + cdim].astype(BF16)
            w_dt = jnp.pad(w_in[:, d_inner + cdim:], ((0, 0), (0, LANES - heads))).astype(BF16)
            w_out = ssd_w_out[j].astype(BF16)
            sp = (ssd_conv_w[j], ssd_conv_b[j], ssd_dt_bias[j], ssd_a_log[j], ssd_d[j])
            z_p, hm_p, dt_p = _projection("normmod", "none", (xp, norm1_g[l], mp[1], mp[0]), w_z, tm=TM_PROMPT,
                                          tn=TN_PROJ, rows_per_mod=seq, w_extra=w_dt, emit_lhs=True, name="ssd_in_z")
            z_s, hm_s, dt_s = _projection("normmod", "none", (xs, norm1_g[l], ms[1], ms[0]), w_z, tm=tms,
                                          tn=TN_PROJ, rows_per_mod=1, w_extra=w_dt, emit_lhs=True, name="ssd_in_z")
            xbc_p = _projection("plain", "none", (hm_p,), w_xbc, tm=TM_PROMPT, tn=TN_PROJ, name="ssd_in_xbc")
            xbc_s = _projection("plain", "none", (hm_s,), w_xbc, tm=tms, tn=TN_PROJ, name="ssd_in_xbc")
            xbc_p3 = xbc_p.reshape(bp, seq, cdim)
            yn_p, st_p = _ssd_seq(xbc_p3, dt_p.reshape(bp, seq, LANES), z_p.reshape(bp, seq, d_inner), *sp,
                                  ssd_norm_g[j], d_inner)
            y_s, st_s = _ssd_step(xbc_s, state_ssd_conv[j], dt_s, state_ssd[j], *sp, d_inner)
            xp = _projection("plain", "resid", (yn_p.reshape(tp, d_inner),), w_out, tm=TM_PROMPT, tn=TN_PROJ,
                             rows_per_mod=seq, resid=(xp, mp[2]), name="ssd_out")
            xs = _projection("gated", "resid", (y_s, z_s, ssd_norm_g[j]), w_out, tm=tms, tn=512,
                             rows_per_mod=1, resid=(xs, ms[2]), name="ssd_out")
            g_, n_ = SSD_GROUPS, SSD_STATE
            outs.setdefault("ssd_conv_p", []).append(xbc_p3[:, seq - (CONV_W - 1):])
            outs.setdefault("ssd_p", []).append(
                st_p.reshape(bp, g_, n_, heads // g_, SSD_HEAD_DIM).transpose(0, 1, 3, 4, 2))
            outs.setdefault("ssd_conv_s", []).append(
                jnp.concatenate([state_ssd_conv[j][:, 1:], xbc_s[:, None]], axis=1))
            outs.setdefault("ssd_s", []).append(st_s.reshape(state_ssd[j].shape))

        xp, xs = _peer_layer(
            [dict(x=xp, sc=mp[4], sh=mp[3], gate=mp[5], rows_per_mod=seq, tm=TM_PROMPT, tme=TM_EXPERT, te=TE_EXPERT),
             dict(x=xs, sc=ms[4], sh=ms[3], gate=ms[5], rows_per_mod=1, tm=tms, tme=tms, te=TE_EXPERT_SAMPLE)],
            peer_w_q[l], peer_k1[l], peer_k2[l], u_bf, vt_bf, l, norm2_g[l],
            final_g=final_g if l == depth - 1 else None)

    y_p = xp.reshape(bp, seq, d)
    y_s = xs.reshape(bs, 1, d)
    st = lambda name: outs[name][0][None] if len(outs[name]) == 1 else jnp.stack(outs[name])
    return (y_p, y_s, st("rg_conv_p"), st("rg_h_p"), st("ssd_conv_p"), st("ssd_p"),
            st("rg_conv_s"), st("rg_h_s"), st("ssd_conv_s"), st("ssd_s"))
```
